```python
import math
import jax, jax.numpy as jnp
from jax import lax
import numpy as np

D_MODEL = 2048
BATCH = 2
SEQ = 4096
DEPTH = 2

N_META = 16
DN_HEADS = 16
DN_DK = 128
DN_DV = 128
SHORT_CONV = 4
CHUNK = 64
META_PAD = (-N_META) % CHUNK
CONV_CH = D_MODEL
CONV_WIDTH = 31
D_FF = 7 * D_MODEL // 2
N_EXPERTS = 8
TOP_K = 2
N_DENSE = (DEPTH + 1) // 2
N_MOE = DEPTH // 2
EPS = 1e-6

Q_DIM = DN_HEADS * DN_DK
V_DIM = DN_HEADS * DN_DV
QKV_DIM = 2 * Q_DIM + V_DIM
OFF_Z = QKV_DIM
OFF_B = OFF_Z + V_DIM
OFF_A = OFF_B + DN_HEADS
OFF_GLU = OFF_A + DN_HEADS
OFF_GATE = OFF_GLU + 2 * CONV_CH
IN_DIM = OFF_GATE + 2 * D_MODEL

kernel_name = "hybrid_gdn_conformer_moe_block"


def rmsnorm(x, w):
    xf = x.astype(jnp.float32)
    y = xf * lax.rsqrt(jnp.mean(xf * xf, axis=-1, keepdims=True) + EPS)
    return (y * w.astype(jnp.float32)).astype(x.dtype)


def layernorm(x, g, b):
    xf = x.astype(jnp.float32)
    mu = jnp.mean(xf, axis=-1, keepdims=True)
    var = jnp.mean(jnp.square(xf - mu), axis=-1, keepdims=True)
    y = (xf - mu) * lax.rsqrt(var + EPS) * g.astype(jnp.float32) + b.astype(jnp.float32)
    return y.astype(x.dtype)


def l2norm(x):
    xf = x.astype(jnp.float32)
    return xf * lax.rsqrt(jnp.sum(xf * xf, axis=-1, keepdims=True) + EPS)


def causal_depthwise_conv(x, w):
    k = w.shape[0]
    return lax.conv_general_dilated(
        x, w[:, None, :].astype(x.dtype), window_strides=(1,), padding=[(k - 1, 0)],
        dimension_numbers=("NWC", "WIO", "NWC"), feature_group_count=x.shape[-1])


def chunk_gated_delta_rule(q, k, v, g, beta):
    c = q.shape[-2]
    causal = jnp.tril(jnp.ones((c, c), dtype=bool))
    strict = jnp.tril(jnp.ones((c, c), dtype=bool), -1)
    g_cum = jnp.cumsum(g, axis=-1)
    decay = jnp.exp(jnp.where(causal, g_cum[..., :, None] - g_cum[..., None, :], -jnp.inf))
    k_beta = k * beta[..., None]
    kk = jnp.where(strict, jnp.einsum("bhncd,bhnsd->bhncs", k_beta, k) * decay, 0.0)
    rhs = jnp.concatenate([v * beta[..., None], k_beta * jnp.exp(g_cum)[..., None]], axis=-1)
    sol = lax.linalg.triangular_solve(kk, rhs, left_side=True, lower=True, unit_diagonal=True)
    u, w = sol[..., :DN_DV], sol[..., DN_DV:]
    qk = jnp.einsum("bhncd,bhnsd->bhncs", q, k) * decay
    q_dec = q * jnp.exp(g_cum)[..., None]
    k_dec = k * jnp.exp(g_cum[..., -1:] - g_cum)[..., None]
    chunk_decay = jnp.exp(g_cum[..., -1])

    def step(state, xs):
        qk_c, qd_c, kd_c, u_c, w_c, cd_c = xs
        v_new = u_c - jnp.einsum("bhck,bhkv->bhcv", w_c, state)
        o = jnp.einsum("bhck,bhkv->bhcv", qd_c, state) + jnp.einsum("bhcs,bhsv->bhcv", qk_c, v_new)
        state = state * cd_c[..., None, None] + jnp.einsum("bhck,bhcv->bhkv", kd_c, v_new)
        return state, o

    xs = tuple(jnp.moveaxis(t, 2, 0) for t in (qk, q_dec, k_dec, u, w, chunk_decay))
    b, h = q.shape[0], q.shape[1]
    s0 = jnp.zeros((b, h, DN_DK, DN_DV), jnp.float32)
    _, o = lax.scan(step, s0, xs)
    return jnp.moveaxis(o, 0, 2)


def gated_deltanet(qkv, z, b_in, a_in, short_conv_w, a_log, dt_bias, dn_norm, w_dn_out):
    bsz, length, _ = qkv.shape
    qkv = jax.nn.silu(causal_depthwise_conv(qkv, short_conv_w))
    q, k, v = jnp.split(qkv, [Q_DIM, 2 * Q_DIM], axis=-1)
    q = l2norm(q.reshape(bsz, length, DN_HEADS, DN_DK)) * (DN_DK ** -0.5)
    k = l2norm(k.reshape(bsz, length, DN_HEADS, DN_DK))
    v = v.reshape(bsz, length, DN_HEADS, DN_DV).astype(jnp.float32)
    beta = jax.nn.sigmoid(b_in.astype(jnp.float32))
    g = -jnp.exp(a_log.astype(jnp.float32)) * jax.nn.softplus(
        a_in.astype(jnp.float32) + dt_bias.astype(jnp.float32))
    q, k, v, beta, g = [jnp.pad(t, [(0, 0), (METAPAD_, 0)] + [(0, 0)] * (t.ndim - 2)) if False else
                        jnp.pad(t, [(0, 0), (META_PAD, 0)] + [(0, 0)] * (t.ndim - 2))
                        for t, METAPAD_ in ((q, 0), (k, 0), (v, 0), (beta, 0), (g, 0))]
    n_chunks = (length + META_PAD) // CHUNK

    def to_chunks(t):
        return t.reshape(bsz, n_chunks, CHUNK, DN_HEADS, -1).transpose(0, 3, 1, 2, 4)

    o = chunk_gated_delta_rule(to_chunks(q), to_chunks(k), to_chunks(v),
                               to_chunks(g[..., None])[..., 0], to_chunks(beta[..., None])[..., 0])
    o = o.transpose(0, 2, 3, 1, 4).reshape(bsz, n_chunks * CHUNK, DN_HEADS, DN_DV)[:, META_PAD:]
    zf = z.reshape(bsz, length, DN_HEADS, DN_DV).astype(jnp.float32)
    o = rmsnorm(o, dn_norm) * jax.nn.silu(zf)
    return o.reshape(bsz, length, V_DIM).astype(qkv.dtype) @ w_dn_out


def conformer_conv(glu_in, dw_conv_w, dw_conv_b, conv_ln_g, conv_ln_b, w_conv_out):
    a, gate = jnp.split(glu_in, 2, axis=-1)
    y = a * jax.nn.sigmoid(gate)
    y = causal_depthwise_conv(y, dw_conv_w) + dw_conv_b
    y = jax.nn.silu(layernorm(y, conv_ln_g, conv_ln_b))
    return y @ w_conv_out


def hybrid_mixer(u, w_in, short_conv_w, a_log, dt_bias, dn_norm, w_dn_out,
                 dw_conv_w, dw_conv_b, conv_ln_g, conv_ln_b, w_conv_out, w_merge_out):
    proj = u @ w_in
    qkv, z, b_in, a_in, glu_in, gates = jnp.split(
        proj, [OFF_Z, OFF_B, OFF_A, OFF_GLU, OFF_GATE], axis=-1)
    y_dn = gated_deltanet(qkv, z, b_in, a_in, short_conv_w, a_log, dt_bias, dn_norm, w_dn_out)
    y_cv = conformer_conv(glu_in, dw_conv_w, dw_conv_b, conv_ln_g, conv_ln_b, w_conv_out)
    g_dn, g_cv = jnp.split(jax.nn.sigmoid(gates), 2, axis=-1)
    return (g_dn * y_dn + g_cv * y_cv) @ w_merge_out


def swiglu(u, w1, w3, w2):
    return (jax.nn.silu(u @ w1) * (u @ w3)) @ w2


def moe_swiglu(u, router_w, w1, w3, w2):
    logits = jnp.einsum("bld,de->ble", u.astype(jnp.float32), router_w.astype(jnp.float32))
    top_vals, top_idx = lax.top_k(logits, TOP_K)
    top_w = jax.nn.softmax(top_vals, axis=-1)
    combine = jnp.sum(jax.nn.one_hot(top_idx, N_EXPERTS, dtype=jnp.float32) * top_w[..., None], axis=-2)
    out = jnp.zeros_like(u)
    for e in range(N_EXPERTS):
        out = out + combine[..., e:e + 1].astype(u.dtype) * swiglu(u, w1[e], w3[e], w2[e])
    return out


def setup_inputs(seed: int = 0) -> dict:
    key = jax.random.key(seed)
    ks = jax.random.split(key, 24)
    f32 = jnp.float32

    def nrm(k, shape, fan_in):
        return jax.random.normal(k, shape, f32) * (fan_in ** -0.5)

    def gain(k, shape):
        return 1.0 + 0.02 * jax.random.normal(k, shape, f32)

    dt = jnp.exp(jax.random.uniform(ks[6], (DEPTH, DN_HEADS), f32, minval=math.log(1e-3), maxval=math.log(1e-1)))
    return {
        "x": jax.random.normal(ks[0], (BATCH, SEQ, D_MODEL), f32),
        "meta_tokens": jax.random.normal(ks[1], (N_META, D_MODEL), f32),
        "attn_norm": gain(ks[2], (DEPTH, D_MODEL)),
        "w_in": nrm(ks[3], (DEPTH, D_MODEL, IN_DIM), D_MODEL),
        "short_conv_w": nrm(ks[4], (DEPTH, SHORT_CONV, QKV_DIM), SHORT_CONV),
        "a_log": jnp.log(jax.random.uniform(ks[5], (DEPTH, DN_HEADS), f32, minval=1.0, maxval=16.0)),
        "dt_bias": dt + jnp.log(-jnp.expm1(-dt)),
        "dn_norm": gain(ks[7], (DEPTH, DN_DV)),
        "w_dn_out": nrm(ks[8], (DEPTH, V_DIM, D_MODEL), V_DIM),
        "dw_conv_w": nrm(ks[9], (DEPTH, CONV_WIDTH, CONV_CH), CONV_WIDTH),
        "dw_conv_b": 0.01 * jax.random.normal(ks[10], (DEPTH, CONV_CH), f32),
        "conv_ln_g": gain(ks[11], (DEPTH, CONV_CH)),
        "conv_ln_b": 0.02 * jax.random.normal(ks[12], (DEPTH, CONV_CH), f32),
        "w_conv_out": nrm(ks[13], (DEPTH, CONV_CH, D_MODEL), CONV_CH),
        "w_merge_out": nrm(ks[14], (DEPTH, D_MODEL, D_MODEL), D_MODEL),
        "ffn_norm": gain(ks[15], (DEPTH, D_MODEL)),
        "dense_w1": nrm(ks[16], (N_DENSE, D_MODEL, D_FF), D_MODEL),
        "dense_w3": nrm(ks[17], (N_DENSE, D_MODEL, D_FF), D_MODEL),
        "dense_w2": nrm(ks[18], (N_DENSE, D_FF, D_MODEL), D_FF),
        "router_w": nrm(ks[19], (N_MOE, D_MODEL, N_EXPERTS), D_MODEL),
        "moe_w1": nrm(ks[20], (N_MOE, N_EXPERTS, D_MODEL, D_FF), D_MODEL),
        "moe_w3": nrm(ks[21], (N_MOE, N_EXPERTS, D_MODEL, D_FF), D_MODEL),
        "moe_w2": nrm(ks[22], (N_MOE, N_EXPERTS, D_FF, D_MODEL), D_FF),
        "final_norm": gain(ks[23], (D_MODEL,)),
    }


def reference(x, meta_tokens, attn_norm, w_in, short_conv_w, a_log, dt_bias, dn_norm, w_dn_out,
              dw_conv_w, dw_conv_b, conv_ln_g, conv_ln_b, w_conv_out, w_merge_out, ffn_norm,
              dense_w1, dense_w3, dense_w2, router_w, moe_w1, moe_w3, moe_w2, final_norm):
    meta = jnp.broadcast_to(meta_tokens[None].astype(x.dtype), (x.shape[0], N_META, D_MODEL))
    h = jnp.concatenate([meta, x], axis=1)
    for i in range(DEPTH):
        u = rmsnorm(h, attn_norm[i])
        h = h + hybrid_mixer(u, w_in[i], short_conv_w[i], a_log[i], dt_bias[i], dn_norm[i], w_dn_out[i],
                             dw_conv_w[i], dw_conv_b[i], conv_ln_g[i], conv_ln_b[i], w_conv_out[i],
                             w_merge_out[i])
        u = rmsnorm(h, ffn_norm[i])
        if i % 2 == 0:
            j = i // 2
            h = h + swiglu(u, dense_w1[j], dense_w3[j], dense_w2[j])
        else:
            j = i // 2
            h = h + moe_swiglu(u, router_w[j], moe_w1[j], moe_w3[j], moe_w2[j])
    return rmsnorm(h, final_norm)[:, N_META:]
```

```python
import functools

import jax
import jax.numpy as jnp
from jax import lax
from jax.experimental import pallas as pl
from jax.experimental.pallas import tpu as pltpu

DN_DK = 128
DN_DV = 128
CHUNK = 64
EPS = 1e-6
LANES = 128
NEG_BIG = -1e30
V7X_VMEM_BYTES = 64 * 1024 * 1024
VMEM_CAP = 56 * 1024 * 1024

F32 = jnp.float32
BF16 = jnp.bfloat16
HI = lax.Precision.HIGHEST


def _pick(n, target, mult):
    best = None
    for d in range(mult, min(n, target) + 1, mult):
        if n % d == 0:
            best = d
    assert best is not None, (n, target, mult)
    return best


def _params(sem, vmem_bytes):
    return pltpu.CompilerParams(dimension_semantics=sem,
                                vmem_limit_bytes=int(min(VMEM_CAP, max(vmem_bytes, 16 * 1024 * 1024))))


def _sigmoid(x):
    return 1.0 / (1.0 + jnp.exp(-x))


def _silu(x):
    return x * _sigmoid(x)


def _softplus(x):
    return jnp.maximum(x, 0.0) + jnp.log(1.0 + jnp.exp(-jnp.abs(x)))


def _addnorm_kernel(*refs, lp, pad, has_delta, want_h):
    it = iter(refs)
    h_ref = next(it)
    d_ref = next(it) if has_delta else None
    w_ref = next(it)
    hn_ref = next(it) if want_h else None
    u_ref = next(it)
    tr = h_ref.shape[0]
    h = h_ref[...]
    if has_delta:
        h = h + d_ref[...]
    if want_h:
        hn_ref[...] = h
    y = h * lax.rsqrt(jnp.mean(h * h, axis=-1, keepdims=True) + EPS) * w_ref[...]
    row = pl.program_id(0) * tr + lax.broadcasted_iota(jnp.int32, (tr, 1), 0)
    y = jnp.where(row % lp >= pad, y, 0.0)
    u_ref[...] = y.astype(u_ref.dtype)


def _addnorm(h, delta, w, *, lp, pad, want_h):
    m, d = h.shape
    tr = _pick(m, 520, 8)
    has_delta = delta is not None
    row_spec = pl.BlockSpec((tr, d), lambda i: (i, 0))
    in_specs = [row_spec] + ([row_spec] if has_delta else []) + [pl.BlockSpec((1, d), lambda i: (0, 0))]
    out_shape = ([jax.ShapeDtypeStruct((m, d), F32)] if want_h else []) + [jax.ShapeDtypeStruct((m, d), BF16)]
    out_specs = ([row_spec] if want_h else []) + [row_spec]
    args = [h] + ([delta] if has_delta else []) + [w.reshape(1, d)]
    outs = pl.pallas_call(
        functools.partial(_addnorm_kernel, lp=lp, pad=pad, has_delta=has_delta, want_h=want_h),
        grid=(m // tr,), in_specs=in_specs, out_specs=out_specs, out_shape=out_shape,
        compiler_params=_params(("parallel",), 12 * tr * d * 4),
        name="addnorm",
    )(*args)
    return outs if want_h else (None, outs[0])


def _final_norm_kernel(h_ref, d_ref, w_ref, o_ref):
    h = h_ref[0] + d_ref[0]
    o_ref[0] = h * lax.rsqrt(jnp.mean(h * h, axis=-1, keepdims=True) + EPS) * w_ref[...]


def _final_norm(h, delta, w, *, skip):
    b, lp, d = h.shape
    seq = lp - skip
    tr = skip
    assert seq % tr == 0
    in_spec = pl.BlockSpec((1, tr, d), lambda bi, i: (bi, i + 1, 0))
    return pl.pallas_call(
        _final_norm_kernel,
        grid=(b, seq // tr),
        in_specs=[in_spec, in_spec, pl.BlockSpec((1, d), lambda bi, i: (0, 0))],
        out_specs=pl.BlockSpec((1, tr, d), lambda bi, i: (bi, i, 0)),
        out_shape=jax.ShapeDtypeStruct((b, seq, d), F32),
        compiler_params=_params(("parallel", "parallel"), 12 * tr * d * 4),
        name="final_norm",
    )(h, delta, w.reshape(1, d))


def _mm_kernel(x_ref, w_ref, o_ref):
    o_ref[...] = jnp.dot(x_ref[...], w_ref[0].astype(BF16), preferred_element_type=F32).astype(o_ref.dtype)


def _mm(x, w, layer, *, n_out=None, tn=512, out_dtype=F32, name="mm"):
    m, k = x.shape
    n_out = w.shape[2] if n_out is None else n_out
    tn = min(tn, n_out)
    assert n_out % tn == 0
    tm = _pick(m, 1040, 16)
    vmem = 2 * (tm * k * 2 + k * tn * 4 + tm * tn * 4) + k * tn * 2 + tm * tn * 4
    return pl.pallas_call(
        _mm_kernel,
        grid=(m // tm, n_out // tn),
        in_specs=[pl.BlockSpec((tm, k), lambda i, j: (i, 0)),
                  pl.BlockSpec((1, k, tn), lambda i, j: (layer, 0, j))],
        out_specs=pl.BlockSpec((tm, tn), lambda i, j: (i, j)),
        out_shape=jax.ShapeDtypeStruct((m, n_out), out_dtype),
        compiler_params=_params(("parallel", "arbitrary"), vmem + (4 << 20)),
        name=name,
    )(x, w)


def _gdn_prep_kernel(q_ref, k_ref, v_ref, qh_ref, kh_ref, vh_ref, ba_ref, cw_ref, gp_ref,
                     u_ref, w_ref, qd_ref, kd_ref, qk_ref, gc_ref, ext_ref, *, heads, pad):
    c = CHUNK
    hd = heads * DN_DK
    n = pl.program_id(1)

    def conv_silu(main_ref, halo_ref, part):
        ext_ref[0:8, :] = halo_ref[0]
        ext_ref[8:8 + c, :] = main_ref[0]
        kw = cw_ref.shape[0]
        acc = jnp.zeros((c, hd), F32)
        for j in range(kw):
            acc = acc + cw_ref[j:j + 1, part * hd:(part + 1) * hd] * ext_ref[8 - (kw - 1) + j:8 - (kw - 1) + j + c, :]
        return _silu(acc)

    row = n * c + lax.broadcasted_iota(jnp.int32, (c, 1), 0)
    live = row >= pad
    beta = jnp.where(live, _sigmoid(ba_ref[0, :, 0:LANES]), 0.0)
    g = jnp.where(live, -jnp.exp(gp_ref[0:1, :]) * _softplus(ba_ref[0, :, LANES:2 * LANES] + gp_ref[1:2, :]), 0.0)

    ii = lax.broadcasted_iota(jnp.int32, (2 * c, c), 0)
    jj = lax.broadcasted_iota(jnp.int32, (2 * c, c), 1)
    tri_pad = jnp.where((ii >= jj) & (ii < c), 1.0, 0.0).astype(F32)
    gc_pad = jnp.dot(tri_pad, g, precision=HI, preferred_element_type=F32)
    gc = gc_pad[0:c]
    gct = gc_pad.T
    gc_ref[0] = gc
    eg = jnp.exp(gc)
    erev = jnp.exp(gc[c - 1:c, :] - gc)

    i2 = lax.broadcasted_iota(jnp.int32, (c, 2 * c), 0)
    j2 = lax.broadcasted_iota(jnp.int32, (c, 2 * c), 1)
    causal = i2 >= j2
    i1 = lax.broadcasted_iota(jnp.int32, (c, c), 0)
    j1 = lax.broadcasted_iota(jnp.int32, (c, c), 1)
    strict = i1 > j1
    eye = jnp.where(i1 == j1, 1.0, 0.0).astype(F32)

    qa = conv_silu(q_ref, qh_ref, 0)
    ka = conv_silu(k_ref, kh_ref, 1)
    va = conv_silu(v_ref, vh_ref, 2)

    def hdot(a, b):
        return jnp.dot(a, b, precision=HI, preferred_element_type=F32)

    for h in range(heads):
        sl = slice(h * DN_DK, (h + 1) * DN_DK)
        qh = qa[:, sl]
        kh = ka[:, sl]
        vh = va[:, sl]
        qh = qh * lax.rsqrt(jnp.sum(qh * qh, axis=-1, keepdims=True) + EPS) * (DN_DK ** -0.5)
        kh = kh * lax.rsqrt(jnp.sum(kh * kh, axis=-1, keepdims=True) + EPS)
        bh = beta[:, h:h + 1]
        egh = eg[:, h:h + 1]
        kb = kh * bh
        decay = jnp.exp(jnp.where(causal, gc[:, h:h + 1] - gct[h:h + 1, :], NEG_BIG))
        k_pad = jnp.concatenate([kh, jnp.zeros_like(kh)], axis=0)
        lhs = jnp.concatenate([kb, qh], axis=0).astype(BF16)
        kq = lax.dot_general(lhs, k_pad.astype(BF16), (((1,), (1,)), ((), ())), preferred_element_type=F32)
        a = jnp.where(strict, kq[0:c, 0:c] * decay[:, 0:c], 0.0)
        qk_ref[0, :, h * 2 * c:(h + 1) * 2 * c] = kq[c:2 * c, :] * decay
        t = eye - a
        p = a
        span = 2
        while span < c:
            p = hdot(p, p)
            t = t + hdot(t, p)
            span *= 2
        rhs = jnp.concatenate([vh * bh, kb * egh], axis=1)
        sol = hdot(t, rhs)
        u_ref[0, :, sl] = sol[:, 0:DN_DV]
        w_ref[0, :, sl] = sol[:, DN_DV:DN_DV + DN_DK]
        qd_ref[0, :, sl] = qh * egh
        kd_ref[0, :, sl] = kh * erev[:, h:h + 1]


def _gdn_prep(qkvz, ba, conv_w, gate_params, *, heads, pad):
    b, lp, _ = qkvz.shape
    hd = heads * DN_DK
    nc = lp // CHUNK
    main = lambda part: pl.BlockSpec((1, CHUNK, hd), lambda bi, n: (bi, n, part))
    halo = lambda part: pl.BlockSpec((1, 8, hd), lambda bi, n: (bi, jnp.maximum(n * (CHUNK // 8) - 1, 0), part))
    out_hd = pl.BlockSpec((1, CHUNK, hd), lambda bi, n: (bi, n, 0))
    f = lambda width: jax.ShapeDtypeStruct((b, lp, width), F32)
    return pl.pallas_call(
        functools.partial(_gdn_prep_kernel, heads=heads, pad=pad),
        grid=(b, nc),
        in_specs=[main(0), main(1), main(2), halo(0), halo(1), halo(2),
                  pl.BlockSpec((1, CHUNK, 2 * LANES), lambda bi, n: (bi, n, 0)),
                  pl.BlockSpec(conv_w.shape, lambda bi, n: (0, 0)),
                  pl.BlockSpec(gate_params.shape, lambda bi, n: (0, 0))],
        out_specs=[out_hd, out_hd, out_hd, out_hd,
                   pl.BlockSpec((1, CHUNK, heads * 2 * CHUNK), lambda bi, n: (bi, n, 0)),
                   pl.BlockSpec((1, CHUNK, LANES), lambda bi, n: (bi, n, 0))],
        out_shape=[f(hd), f(hd), f(hd), f(hd), f(heads * 2 * CHUNK), f(LANES)],
        scratch_shapes=[pltpu.VMEM((8 + CHUNK, hd), F32)],
        compiler_params=_params(("parallel", "arbitrary"), 40 * CHUNK * hd * 4),
        name="gdn_prep",
    )(qkvz, qkvz, qkvz, qkvz, qkvz, qkvz, ba, conv_w, gate_params)


def _gdn_scan_kernel(u_ref, w_ref, qd_ref, kd_ref, qk_ref, gc_ref, z_ref, nw_ref, o_ref, s_ref, *, heads):
    c = CHUNK

    @pl.when(pl.program_id(1) == 0)
    def _():
        s_ref[...] = jnp.zeros_like(s_ref)

    cd = jnp.exp(gc_ref[0, c - 1:c, :])
    for h in range(heads):
        sl = slice(h * DN_DV, (h + 1) * DN_DV)
        s = s_ref[h]
        s_b = s.astype(BF16)
        v_new = u_ref[0, :, sl] - jnp.dot(w_ref[0, :, sl].astype(BF16), s_b, preferred_element_type=F32)
        v_b = v_new.astype(BF16)
        o = (jnp.dot(qd_ref[0, :, sl].astype(BF16), s_b, preferred_element_type=F32)
             + jnp.dot(qk_ref[0, :, h * 2 * c:h * 2 * c + c].astype(BF16), v_b, preferred_element_type=F32))
        s_ref[h] = s * cd[:, h:h + 1] + lax.dot_general(
            kd_ref[0, :, sl].astype(BF16), v_b, (((0,), (0,)), ((), ())), preferred_element_type=F32)
        on = o * lax.rsqrt(jnp.mean(o * o, axis=-1, keepdims=True) + EPS) * nw_ref[...]
        o_ref[0, :, sl] = (on * _silu(z_ref[0, :, sl])).astype(o_ref.dtype)


def _gdn_scan(u, w, qd, kd, qk, gc, qkvz, norm_w, *, heads):
    b, lp, hd = u.shape
    nc = lp // CHUNK
    blk = lambda width, col: pl.BlockSpec((1, CHUNK, width), lambda bi, n: (bi, n, col))
    return pl.pallas_call(
        functools.partial(_gdn_scan_kernel, heads=heads),
        grid=(b, nc),
        in_specs=[blk(hd, 0), blk(hd, 0), blk(hd, 0), blk(hd, 0), blk(heads * 2 * CHUNK, 0), blk(LANES, 0),
                  blk(hd, 3), pl.BlockSpec((1, DN_DV), lambda bi, n: (0, 0))],
        out_specs=blk(hd, 0),
        out_shape=jax.ShapeDtypeStruct((b, lp, hd), BF16),
        scratch_shapes=[pltpu.VMEM((heads, DN_DK, DN_DV), F32)],
        compiler_params=_params(("parallel", "arbitrary"), 24 * CHUNK * hd * 4),
        name="gdn_scan",
    )(u, w, qd, kd, qk, gc, qkvz, norm_w.reshape(1, DN_DV))


CONV_HALO = 32
CONV_CT = 256
CONV_RT = 64


def _conformer_kernel(a_ref, g_ref, ah_ref, gh_ref, cw_ref, cb_ref, lg_ref, lb_ref, o_ref, ext_ref, y_ref):
    r, ch = o_ref.shape[1], o_ref.shape[2]
    kw = cw_ref.shape[0]
    ext_ref[0:CONV_HALO, :] = ah_ref[0] * _sigmoid(gh_ref[0])
    ext_ref[CONV_HALO:CONV_HALO + r, :] = a_ref[0] * _sigmoid(g_ref[0])
    base = CONV_HALO - (kw - 1)

    def ch_body(ci, carry):
        c0 = pl.multiple_of(ci * CONV_CT, CONV_CT)
        for r0 in range(0, r, CONV_RT):
            acc = jnp.broadcast_to(cb_ref[:, pl.ds(c0, CONV_CT)], (CONV_RT, CONV_CT))
            for j in range(kw):
                acc = acc + cw_ref[j:j + 1, pl.ds(c0, CONV_CT)] * ext_ref[base + r0 + j:base + r0 + j + CONV_RT, pl.ds(c0, CONV_CT)]
            y_ref[r0:r0 + CONV_RT, pl.ds(c0, CONV_CT)] = acc
        return carry

    lax.fori_loop(0, ch // CONV_CT, ch_body, 0)
    y = y_ref[...]
    mu = jnp.mean(y, axis=-1, keepdims=True)
    yc = y - mu
    var = jnp.mean(yc * yc, axis=-1, keepdims=True)
    yn = yc * lax.rsqrt(var + EPS) * lg_ref[...] + lb_ref[...]
    o_ref[0] = _silu(yn).astype(o_ref.dtype)


def _conformer(tail, conv_w, conv_b, ln_g, ln_b):
    b, lp, _ = tail.shape
    ch = conv_w.shape[1]
    assert conv_w.shape[0] - 1 <= CONV_HALO and ch % CONV_CT == 0
    r = _pick(lp, 320, CONV_RT)
    assert r % CONV_HALO == 0
    main = lambda col: pl.BlockSpec((1, r, ch), lambda bi, i: (bi, i, col))
    halo = lambda col: pl.BlockSpec((1, CONV_HALO, ch), lambda bi, i: (bi, jnp.maximum(i * (r // CONV_HALO) - 1, 0), col))
    vec = pl.BlockSpec((1, ch), lambda bi, i: (0, 0))
    return pl.pallas_call(
        _conformer_kernel,
        grid=(b, lp // r),
        in_specs=[main(0), main(1), halo(0), halo(1), pl.BlockSpec(conv_w.shape, lambda bi, i: (0, 0)), vec, vec, vec],
        out_specs=pl.BlockSpec((1, r, ch), lambda bi, i: (bi, i, 0)),
        out_shape=jax.ShapeDtypeStruct((b, lp, ch), BF16),
        scratch_shapes=[pltpu.VMEM((CONV_HALO + r, ch), F32), pltpu.VMEM((r, ch), F32)],
        compiler_params=_params(("parallel", "arbitrary"), 14 * r * ch * 4),
        name="conformer",
    )(tail, tail, tail, tail, conv_w, conv_b.reshape(1, ch), ln_g.reshape(1, ch), ln_b.reshape(1, ch))


def _branch_kernel(od_ref, oc_ref, wd_ref, wc_ref, gd_ref, gc_ref, o_ref):
    yd = jnp.dot(od_ref[...], wd_ref[0].astype(BF16), preferred_element_type=F32)
    yc = jnp.dot(oc_ref[...], wc_ref[0].astype(BF16), preferred_element_type=F32)
    o_ref[...] = (_sigmoid(gd_ref[...]) * yd + _sigmoid(gc_ref[...]) * yc).astype(o_ref.dtype)


def _branches(o_dn, o_cv, w_dn_out, w_conv_out, layer, tail, gate_col0):
    m, k = o_dn.shape
    d = w_dn_out.shape[2]
    assert o_cv.shape == o_dn.shape and w_conv_out.shape == w_dn_out.shape
    tn = min(512, d)
    tm = _pick(m, 1040, 16)
    g0 = gate_col0 // tn
    assert gate_col0 % tn == 0 and d % tn == 0
    x_spec = pl.BlockSpec((tm, k), lambda i, j: (i, 0))
    w_spec = pl.BlockSpec((1, k, tn), lambda i, j: (layer, 0, j))
    vmem = 2 * (2 * tm * k * 2 + 2 * k * tn * 4 + 2 * tm * tn * 4 + tm * tn * 2) + 2 * k * tn * 2 + 4 * tm * tn * 4
    return pl.pallas_call(
        _branch_kernel,
        grid=(m // tm, d // tn),
        in_specs=[x_spec, x_spec, w_spec, w_spec,
                  pl.BlockSpec((tm, tn), lambda i, j: (i, g0 + j)),
                  pl.BlockSpec((tm, tn), lambda i, j: (i, g0 + d // tn + j))],
        out_specs=pl.BlockSpec((tm, tn), lambda i, j: (i, j)),
        out_shape=jax.ShapeDtypeStruct((m, d), BF16),
        compiler_params=_params(("parallel", "arbitrary"), vmem + (4 << 20)),
        name="branches",
    )(o_dn, o_cv, w_dn_out, w_conv_out, tail, tail)


def _ffn_kernel(*refs, n_exp):
    if n_exp > 1:
        x_ref, w1_ref, w3_ref, w2_ref, cmb_ref, o_ref = refs
    else:
        x_ref, w1_ref, w3_ref, w2_ref, o_ref = refs
    e = pl.program_id(1)
    f = pl.program_id(2)

    @pl.when((e == 0) & (f == 0))
    def _():
        o_ref[...] = jnp.zeros_like(o_ref)

    x = x_ref[...]
    h1 = jnp.dot(x, w1_ref[0].astype(BF16), preferred_element_type=F32)
    h3 = jnp.dot(x, w3_ref[0].astype(BF16), preferred_element_type=F32)
    act = (_silu(h1) * h3).astype(BF16)
    y = jnp.dot(act, w2_ref[0].astype(BF16), preferred_element_type=F32)
    if n_exp > 1:
        lane = lax.broadcasted_iota(jnp.int32, cmb_ref.shape, 1)
        y = y * jnp.sum(jnp.where(lane == e, cmb_ref[...], 0.0), axis=-1, keepdims=True)
    o_ref[...] += y


def _ffn(x, w1, w3, w2, e0, n_exp, combine=None):
    m, d = x.shape
    ff = w1.shape[2]
    assert (combine is None) == (n_exp == 1)
    tm = _pick(m, 1040, 16)
    tf = _pick(ff, 256, LANES)
    in_specs = [pl.BlockSpec((tm, d), lambda i, e, f: (i, 0)),
                pl.BlockSpec((1, d, tf), lambda i, e, f: (e0 + e, 0, f)),
                pl.BlockSpec((1, d, tf), lambda i, e, f: (e0 + e, 0, f)),
                pl.BlockSpec((1, tf, d), lambda i, e, f: (e0 + e, f, 0))]
    args = [x, w1, w3, w2]
    if n_exp > 1:
        in_specs.append(pl.BlockSpec((tm, LANES), lambda i, e, f: (i, 0)))
        args.append(combine)
    vmem = 2 * (tm * d * 2 + 3 * d * tf * 4 + tm * d * 4) + 3 * d * tf * 2 + 4 * tm * tf * 4 + tm * d * 4
    return pl.pallas_call(
        functools.partial(_ffn_kernel, n_exp=n_exp),
        grid=(m // tm, n_exp, ff // tf),
        in_specs=in_specs,
        out_specs=pl.BlockSpec((tm, d), lambda i, e, f: (i, 0)),
        out_shape=jax.ShapeDtypeStruct((m, d), F32),
        compiler_params=_params(("parallel", "arbitrary", "arbitrary"), vmem + (4 << 20)),
        name="ffn",
    )(*args)


def _router_kernel(h_ref, d_ref, nw_ref, rw_ref, c_ref, *, n_exp):
    h = h_ref[...] + d_ref[...]
    u = h * lax.rsqrt(jnp.mean(h * h, axis=-1, keepdims=True) + EPS) * nw_ref[...]
    logits = jnp.dot(u, rw_ref[...], precision=HI, preferred_element_type=F32)
    lane = lax.broadcasted_iota(jnp.int32, logits.shape, 1)
    logits = jnp.where(lane < n_exp, logits, NEG_BIG)
    m1 = jnp.max(logits, axis=-1, keepdims=True)
    i1 = jnp.min(jnp.where(logits == m1, lane, LANES), axis=-1, keepdims=True)
    rest = jnp.where(lane == i1, NEG_BIG, logits)
    m2 = jnp.max(rest, axis=-1, keepdims=True)
    i2 = jnp.min(jnp.where(rest == m2, lane, LANES), axis=-1, keepdims=True)
    e2 = jnp.exp(m2 - m1)
    den = 1.0 + e2
    c_ref[...] = jnp.where(lane == i1, 1.0 / den, 0.0) + jnp.where(lane == i2, e2 / den, 0.0)


def _router(h, delta, norm_w, router_w):
    m, d = h.shape
    n_exp = router_w.shape[1]
    assert 2 <= n_exp <= LANES
    rw = jnp.pad(router_w, ((0, 0), (0, LANES - n_exp)))
    tr = _pick(m, 520, 8)
    row_spec = pl.BlockSpec((tr, d), lambda i: (i, 0))
    return pl.pallas_call(
        functools.partial(_router_kernel, n_exp=n_exp),
        grid=(m // tr,),
        in_specs=[row_spec, row_spec, pl.BlockSpec((1, d), lambda i: (0, 0)), pl.BlockSpec((d, LANES), lambda i: (0, 0))],
        out_specs=pl.BlockSpec((tr, LANES), lambda i: (i, 0)),
        out_shape=jax.ShapeDtypeStruct((m, LANES), F32),
        compiler_params=_params(("parallel",), 12 * tr * d * 4),
        name="router",
    )(h, delta, norm_w.reshape(1, d), rw)


def kernel(x, meta_tokens, attn_norm, w_in, short_conv_w, a_log, dt_bias, dn_norm, w_dn_out, dw_conv_w, dw_conv_b,
           conv_ln_g, conv_ln_b, w_conv_out, w_merge_out, ffn_norm, dense_w1, dense_w3, dense_w2, router_w,
           moe_w1, moe_w3, moe_w2, final_norm):
    bsz, seq, d = x.shape
    n_meta = meta_tokens.shape[0]
    depth, heads = a_log.shape
    pad = (-n_meta) % CHUNK
    lp = pad + n_meta + seq
    m = bsz * lp
    hd = heads * DN_DK
    ch = dw_conv_w.shape[2]
    assert lp % CHUNK == 0 and DN_DK == DN_DV == LANES and heads <= LANES
    off_b = 3 * hd + heads * DN_DV
    off_glu = off_b + 2 * heads
    assert w_in.shape[2] == off_glu + 2 * ch + 2 * d

    meta = jnp.broadcast_to(meta_tokens[None].astype(x.dtype), (bsz, n_meta, d))
    h = jnp.concatenate([jnp.zeros((bsz, pad, d), x.dtype), meta, x], axis=1).reshape(m, d)

    delta = None
    for i in range(depth):
        h_new, u = _addnorm(h, delta, attn_norm[i], lp=lp, pad=pad, want_h=delta is not None)
        h = h if delta is None else h_new
        qkvz = _mm(u, w_in, i, n_out=off_b, name="in_proj_qkvz")
        tail = _mm(u, w_in[i:i + 1, :, off_glu:], 0, name="in_proj_tail")
        w_ba = jnp.zeros((1, d, 2 * LANES), F32)
        w_ba = w_ba.at[0, :, 0:heads].set(w_in[i, :, off_b:off_b + heads])
        w_ba = w_ba.at[0, :, LANES:LANES + heads].set(w_in[i, :, off_b + heads:off_glu])
        ba = _mm(u, w_ba, 0, name="in_proj_ba")
        gate_params = jnp.zeros((8, LANES), F32)
        gate_params = gate_params.at[0, 0:heads].set(a_log[i]).at[1, 0:heads].set(dt_bias[i])

        qkvz3 = qkvz.reshape(bsz, lp, off_b)
        tail3 = tail.reshape(bsz, lp, 2 * ch + 2 * d)
        uu, ww, qd, kd, qk, gc = _gdn_prep(qkvz3, ba.reshape(bsz, lp, 2 * LANES), short_conv_w[i], gate_params,
                                            heads=heads, pad=pad)
        o_dn = _gdn_scan(uu, ww, qd, kd, qk, gc, qkvz3, dn_norm[i], heads=heads)
        o_cv = _conformer(tail3, dw_conv_w[i], dw_conv_b[i], conv_ln_g[i], conv_ln_b[i])
        merged = _branches(o_dn.reshape(m, hd), o_cv.reshape(m, ch), w_dn_out, w_conv_out, i, tail, 2 * ch)
        delta = _mm(merged, w_merge_out, i, name="merge_out")

        j = i // 2
        if i % 2 == 0:
            h, u = _addnorm(h, delta, ffn_norm[i], lp=lp, pad=0, want_h=True)
            delta = _ffn(u, dense_w1, dense_w3, dense_w2, j, 1)
        else:
            n_exp = moe_w1.shape[1]
            flat = lambda t: t.reshape((-1,) + t.shape[2:])
            combine = _router(h, delta, ffn_norm[i], router_w[j])
            h, u = _addnorm(h, delta, ffn_norm[i], lp=lp, pad=0, want_h=True)
            delta = _ffn(u, flat(moe_w1), flat(moe_w3), flat(moe_w2), j * n_exp, n_exp, combine)

    return _final_norm(h.reshape(bsz, lp, d), delta.reshape(bsz, lp, d), final_norm, skip=pad + n_meta)
```

```python
import functools

import jax
import jax.numpy as jnp
from jax import lax
from jax.experimental import pallas as pl
from jax.experimental.pallas import tpu as pltpu

DN_DK = 128
DN_DV = 128
CHUNK = 64
EPS = 1e-6
LANES = 128
NEG_BIG = -1e30
V7X_VMEM_BYTES = 64 * 1024 * 1024
VMEM_CAP = 56 * 1024 * 1024

F32 = jnp.float32
BF16 = jnp.bfloat16
HI = lax.Precision.HIGHEST


def _pick(n, target, mult):
    best = None
    for d in range(mult, min(n, target) + 1, mult):
        if n % d == 0:
            best = d
    assert best is not None, (n, target, mult)
    return best


def _params(sem, vmem_bytes):
    return pltpu.CompilerParams(dimension_semantics=sem,
                                vmem_limit_bytes=int(min(VMEM_CAP, max(vmem_bytes, 16 * 1024 * 1024))))


def _sigmoid(x):
    return 1.0 / (1.0 + jnp.exp(-x))


def _silu(x):
    return x * _sigmoid(x)


def _softplus(x):
    return jnp.maximum(x, 0.0) + jnp.log(1.0 + jnp.exp(-jnp.abs(x)))


def _addnorm_kernel(*refs, lp, pad, has_delta, want_h):
    it = iter(refs)
    h_ref = next(it)
    d_ref = next(it) if has_delta else None
    w_ref = next(it)
    hn_ref = next(it) if want_h else None
    u_ref = next(it)
    tr = h_ref.shape[0]
    h = h_ref[...]
    if has_delta:
        h = h + d_ref[...]
    if want_h:
        hn_ref[...] = h
    y = h * lax.rsqrt(jnp.mean(h * h, axis=-1, keepdims=True) + EPS) * w_ref[...]
    row = pl.program_id(0) * tr + lax.broadcasted_iota(jnp.int32, (tr, 1), 0)
    y = jnp.where(row % lp >= pad, y, 0.0)
    u_ref[...] = y.astype(u_ref.dtype)


def _addnorm(h, delta, w, *, lp, pad, want_h):
    m, d = h.shape
    tr = _pick(m, 520, 8)
    has_delta = delta is not None
    row_spec = pl.BlockSpec((tr, d), lambda i: (i, 0))
    in_specs = [row_spec] + ([row_spec] if has_delta else []) + [pl.BlockSpec((1, d), lambda i: (0, 0))]
    out_shape = ([jax.ShapeDtypeStruct((m, d), F32)] if want_h else []) + [jax.ShapeDtypeStruct((m, d), BF16)]
    out_specs = ([row_spec] if want_h else []) + [row_spec]
    args = [h] + ([delta] if has_delta else []) + [w.reshape(1, d)]
    outs = pl.pallas_call(
        functools.partial(_addnorm_kernel, lp=lp, pad=pad, has_delta=has_delta, want_h=want_h),
        grid=(m // tr,), in_specs=in_specs, out_specs=out_specs, out_shape=out_shape,
        compiler_params=_params(("parallel",), 12 * tr * d * 4),
        name="addnorm",
    )(*args)
    return outs if want_h else (None, outs[0])


def _final_norm_kernel(h_ref, d_ref, w_ref, o_ref):
    h = h_ref[0] + d_ref[0]
    o_ref[0] = h * lax.rsqrt(jnp.mean(h * h, axis=-1, keepdims=True) + EPS) * w_ref[...]


def _final_norm(h, delta, w, *, skip):
    b, lp, d = h.shape
    seq = lp - skip
    tr = skip
    assert seq % tr == 0
    in_spec = pl.BlockSpec((1, tr, d), lambda bi, i: (bi, i + 1, 0))
    return pl.pallas_call(
        _final_norm_kernel,
        grid=(b, seq // tr),
        in_specs=[in_spec, in_spec, pl.BlockSpec((1, d), lambda bi, i: (0, 0))],
        out_specs=pl.BlockSpec((1, tr, d), lambda bi, i: (bi, i, 0)),
        out_shape=jax.ShapeDtypeStruct((b, seq, d), F32),
        compiler_params=_params(("parallel", "parallel"), 12 * tr * d * 4),
        name="final_norm",
    )(h, delta, w.reshape(1, d))


def _mm_kernel(x_ref, w_ref, o_ref):
    o_ref[...] = jnp.dot(x_ref[...], w_ref[0].astype(BF16), preferred_element_type=F32).astype(o_ref.dtype)


def _mm(x, w, layer, *, n_out=None, tn=512, out_dtype=F32, name="mm"):
    m, k = x.shape
    n_out = w.shape[2] if n_out is None else n_out
    tn = min(tn, n_out)
    assert n_out % tn == 0
    tm = _pick(m, 2080, 16)
    vmem = 2 * (tm * k * 2 + k * tn * 4 + tm * tn * 4) + k * tn * 2 + tm * tn * 4
    return pl.pallas_call(
        _mm_kernel,
        grid=(m // tm, n_out // tn),
        in_specs=[pl.BlockSpec((tm, k), lambda i, j: (i, 0)),
                  pl.BlockSpec((1, k, tn), lambda i, j: (layer, 0, j))],
        out_specs=pl.BlockSpec((tm, tn), lambda i, j: (i, j)),
        out_shape=jax.ShapeDtypeStruct((m, n_out), out_dtype),
        compiler_params=_params(("parallel", "arbitrary"), vmem + (4 << 20)),
        name=name,
    )(x, w)


def _gdn_prep_kernel(q_ref, k_ref, v_ref, qh_ref, kh_ref, vh_ref, ba_ref, cw_ref, gp_ref,
                     u_ref, w_ref, qd_ref, kd_ref, qk_ref, gc_ref, ext_ref, *, heads, pad):
    c = CHUNK
    hd = heads * DN_DK
    n = pl.program_id(1)

    def conv_silu(main_ref, halo_ref, part):
        ext_ref[0:8, :] = halo_ref[0]
        ext_ref[8:8 + c, :] = main_ref[0]
        kw = cw_ref.shape[0]
        acc = jnp.zeros((c, hd), F32)
        for j in range(kw):
            acc = acc + cw_ref[j:j + 1, part * hd:(part + 1) * hd] * ext_ref[8 - (kw - 1) + j:8 - (kw - 1) + j + c, :]
        return _silu(acc)

    row = n * c + lax.broadcasted_iota(jnp.int32, (c, 1), 0)
    live = row >= pad
    beta = jnp.where(live, _sigmoid(ba_ref[0, :, 0:LANES]), 0.0)
    g = jnp.where(live, -jnp.exp(gp_ref[0:1, :]) * _softplus(ba_ref[0, :, LANES:2 * LANES] + gp_ref[1:2, :]), 0.0)

    ii = lax.broadcasted_iota(jnp.int32, (2 * c, c), 0)
    jj = lax.broadcasted_iota(jnp.int32, (2 * c, c), 1)
    tri_pad = jnp.where((ii >= jj) & (ii < c), 1.0, 0.0).astype(F32)
    gc_pad = jnp.dot(tri_pad, g, precision=HI, preferred_element_type=F32)
    gc = gc_pad[0:c]
    gct = gc_pad.T
    gc_ref[0] = gc
    eg = jnp.exp(gc)
    erev = jnp.exp(gc[c - 1:c, :] - gc)

    i2 = lax.broadcasted_iota(jnp.int32, (c, 2 * c), 0)
    j2 = lax.broadcasted_iota(jnp.int32, (c, 2 * c), 1)
    causal = i2 >= j2
    i1 = lax.broadcasted_iota(jnp.int32, (c, c), 0)
    j1 = lax.broadcasted_iota(jnp.int32, (c, c), 1)
    strict = i1 > j1

    qa = conv_silu(q_ref, qh_ref, 0)
    ka = conv_silu(k_ref, kh_ref, 1)
    va = conv_silu(v_ref, vh_ref, 2)

    def bdot(a, b):
        return jnp.dot(a.astype(BF16), b.astype(BF16), preferred_element_type=F32)

    a_list, rhs_list = [], []
    for h in range(heads):
        sl = slice(h * DN_DK, (h + 1) * DN_DK)
        qh = qa[:, sl]
        kh = ka[:, sl]
        qh = qh * lax.rsqrt(jnp.sum(qh * qh, axis=-1, keepdims=True) + EPS) * (DN_DK ** -0.5)
        kh = kh * lax.rsqrt(jnp.sum(kh * kh, axis=-1, keepdims=True) + EPS)
        bh = beta[:, h:h + 1]
        egh = eg[:, h:h + 1]
        kb = kh * bh
        decay = jnp.exp(jnp.where(causal, gc[:, h:h + 1] - gct[h:h + 1, :], NEG_BIG))
        k_pad = jnp.concatenate([kh, jnp.zeros_like(kh)], axis=0)
        lhs = jnp.concatenate([kb, qh], axis=0).astype(BF16)
        kq = lax.dot_general(lhs, k_pad.astype(BF16), (((1,), (1,)), ((), ())), preferred_element_type=F32)
        a_list.append(jnp.where(strict, kq[0:c, 0:c] * decay[:, 0:c], 0.0))
        qk_ref[0, :, h * 2 * c:(h + 1) * 2 * c] = kq[c:2 * c, :] * decay
        rhs_list.append(jnp.concatenate([va[:, sl] * bh, kb * egh], axis=1))
        qd_ref[0, :, sl] = qh * egh
        kd_ref[0, :, sl] = kh * erev[:, h:h + 1]

    m_list = [-a for a in a_list]
    p_list = a_list
    span = 2
    while span < c:
        p_list = [bdot(p, p) for p in p_list]
        m_list = [mm + p + bdot(mm, p) for mm, p in zip(m_list, p_list)]
        span *= 2
    for h in range(heads):
        sl = slice(h * DN_DK, (h + 1) * DN_DK)
        sol = rhs_list[h] + bdot(m_list[h], rhs_list[h])
        u_ref[0, :, sl] = sol[:, 0:DN_DV]
        w_ref[0, :, sl] = sol[:, DN_DV:DN_DV + DN_DK]


def _gdn_prep(qkvz, ba, conv_w, gate_params, *, heads, pad):
    b, lp, _ = qkvz.shape
    hd = heads * DN_DK
    nc = lp // CHUNK
    main = lambda part: pl.BlockSpec((1, CHUNK, hd), lambda bi, n: (bi, n, part))
    halo = lambda part: pl.BlockSpec((1, 8, hd), lambda bi, n: (bi, jnp.maximum(n * (CHUNK // 8) - 1, 0), part))
    out_hd = pl.BlockSpec((1, CHUNK, hd), lambda bi, n: (bi, n, 0))
    f = lambda width: jax.ShapeDtypeStruct((b, lp, width), F32)
    return pl.pallas_call(
        functools.partial(_gdn_prep_kernel, heads=heads, pad=pad),
        grid=(b, nc),
        in_specs=[main(0), main(1), main(2), halo(0), halo(1), halo(2),
                  pl.BlockSpec((1, CHUNK, 2 * LANES), lambda bi, n: (bi, n, 0)),
                  pl.BlockSpec(conv_w.shape, lambda bi, n: (0, 0)),
                  pl.BlockSpec(gate_params.shape, lambda bi, n: (0, 0))],
        out_specs=[out_hd, out_hd, out_hd, out_hd,
                   pl.BlockSpec((1, CHUNK, heads * 2 * CHUNK), lambda bi, n: (bi, n, 0)),
                   pl.BlockSpec((1, CHUNK, LANES), lambda bi, n: (bi, n, 0))],
        out_shape=[f(hd), f(hd), f(hd), f(hd), f(heads * 2 * CHUNK), f(LANES)],
        scratch_shapes=[pltpu.VMEM((8 + CHUNK, hd), F32)],
        compiler_params=_params(("parallel", "arbitrary"), 40 * CHUNK * hd * 4),
        name="gdn_prep",
    )(qkvz, qkvz, qkvz, qkvz, qkvz, qkvz, ba, conv_w, gate_params)


def _gdn_scan_kernel(u_ref, w_ref, qd_ref, kd_ref, qk_ref, gc_ref, z_ref, nw_ref, o_ref, s_ref, *, heads):
    c = CHUNK

    @pl.when(pl.program_id(1) == 0)
    def _():
        s_ref[...] = jnp.zeros_like(s_ref)

    cd = jnp.exp(gc_ref[0, c - 1:c, :])
    for h in range(heads):
        sl = slice(h * DN_DV, (h + 1) * DN_DV)
        s = s_ref[h]
        s_b = s.astype(BF16)
        v_new = u_ref[0, :, sl] - jnp.dot(w_ref[0, :, sl].astype(BF16), s_b, preferred_element_type=F32)
        v_b = v_new.astype(BF16)
        o = (jnp.dot(qd_ref[0, :, sl].astype(BF16), s_b, preferred_element_type=F32)
             + jnp.dot(qk_ref[0, :, h * 2 * c:h * 2 * c + c].astype(BF16), v_b, preferred_element_type=F32))
        s_ref[h] = s * cd[:, h:h + 1] + lax.dot_general(
            kd_ref[0, :, sl].astype(BF16), v_b, (((0,), (0,)), ((), ())), preferred_element_type=F32)
        on = o * lax.rsqrt(jnp.mean(o * o, axis=-1, keepdims=True) + EPS) * nw_ref[...]
        o_ref[0, :, sl] = (on * _silu(z_ref[0, :, sl])).astype(o_ref.dtype)


def _gdn_scan(u, w, qd, kd, qk, gc, qkvz, norm_w, *, heads):
    b, lp, hd = u.shape
    nc = lp // CHUNK
    blk = lambda width, col: pl.BlockSpec((1, CHUNK, width), lambda bi, n: (bi, n, col))
    return pl.pallas_call(
        functools.partial(_gdn_scan_kernel, heads=heads),
        grid=(b, nc),
        in_specs=[blk(hd, 0), blk(hd, 0), blk(hd, 0), blk(hd, 0), blk(heads * 2 * CHUNK, 0), blk(LANES, 0),
                  blk(hd, 3), pl.BlockSpec((1, DN_DV), lambda bi, n: (0, 0))],
        out_specs=blk(hd, 0),
        out_shape=jax.ShapeDtypeStruct((b, lp, hd), BF16),
        scratch_shapes=[pltpu.VMEM((heads, DN_DK, DN_DV), F32)],
        compiler_params=_params(("parallel", "arbitrary"), 24 * CHUNK * hd * 4),
        name="gdn_scan",
    )(u, w, qd, kd, qk, gc, qkvz, norm_w.reshape(1, DN_DV))


CONV_HALO = 32
CONV_CT = 256
CONV_RT = 64


def _conformer_kernel(a_ref, g_ref, ah_ref, gh_ref, cw_ref, cb_ref, lg_ref, lb_ref, o_ref, ext_ref, y_ref):
    r, ch = o_ref.shape[1], o_ref.shape[2]
    kw = cw_ref.shape[0]
    ext_ref[0:CONV_HALO, :] = ah_ref[0] * _sigmoid(gh_ref[0])
    ext_ref[CONV_HALO:CONV_HALO + r, :] = a_ref[0] * _sigmoid(g_ref[0])
    base = CONV_HALO - (kw - 1)

    def ch_body(ci, carry):
        c0 = pl.multiple_of(ci * CONV_CT, CONV_CT)
        for r0 in range(0, r, CONV_RT):
            acc = jnp.broadcast_to(cb_ref[:, pl.ds(c0, CONV_CT)], (CONV_RT, CONV_CT))
            for j in range(kw):
                acc = acc + cw_ref[j:j + 1, pl.ds(c0, CONV_CT)] * ext_ref[base + r0 + j:base + r0 + j + CONV_RT, pl.ds(c0, CONV_CT)]
            y_ref[r0:r0 + CONV_RT, pl.ds(c0, CONV_CT)] = acc
        return carry

    lax.fori_loop(0, ch // CONV_CT, ch_body, 0)
    y = y_ref[...]
    mu = jnp.mean(y, axis=-1, keepdims=True)
    yc = y - mu
    var = jnp.mean(yc * yc, axis=-1, keepdims=True)
    yn = yc * lax.rsqrt(var + EPS) * lg_ref[...] + lb_ref[...]
    o_ref[0] = _silu(yn).astype(o_ref.dtype)


def _conformer(tail, conv_w, conv_b, ln_g, ln_b):
    b, lp, _ = tail.shape
    ch = conv_w.shape[1]
    assert conv_w.shape[0] - 1 <= CONV_HALO and ch % CONV_CT == 0
    r = _pick(lp, 320, CONV_RT)
    assert r % CONV_HALO == 0
    main = lambda col: pl.BlockSpec((1, r, ch), lambda bi, i: (bi, i, col))
    halo = lambda col: pl.BlockSpec((1, CONV_HALO, ch), lambda bi, i: (bi, jnp.maximum(i * (r // CONV_HALO) - 1, 0), col))
    vec = pl.BlockSpec((1, ch), lambda bi, i: (0, 0))
    return pl.pallas_call(
        _conformer_kernel,
        grid=(b, lp // r),
        in_specs=[main(0), main(1), halo(0), halo(1), pl.BlockSpec(conv_w.shape, lambda bi, i: (0, 0)), vec, vec, vec],
        out_specs=pl.BlockSpec((1, r, ch), lambda bi, i: (bi, i, 0)),
        out_shape=jax.ShapeDtypeStruct((b, lp, ch), BF16),
        scratch_shapes=[pltpu.VMEM((CONV_HALO + r, ch), F32), pltpu.VMEM((r, ch), F32)],
        compiler_params=_params(("parallel", "arbitrary"), 14 * r * ch * 4),
        name="conformer",
    )(tail, tail, tail, tail, conv_w, conv_b.reshape(1, ch), ln_g.reshape(1, ch), ln_b.reshape(1, ch))


def _branch_kernel(od_ref, oc_ref, wd_ref, wc_ref, gd_ref, gc_ref, o_ref):
    yd = jnp.dot(od_ref[...], wd_ref[0].astype(BF16), preferred_element_type=F32)
    yc = jnp.dot(oc_ref[...], wc_ref[0].astype(BF16), preferred_element_type=F32)
    o_ref[...] = (_sigmoid(gd_ref[...]) * yd + _sigmoid(gc_ref[...]) * yc).astype(o_ref.dtype)


def _branches(o_dn, o_cv, w_dn_out, w_conv_out, layer, tail, gate_col0):
    m, k = o_dn.shape
    d = w_dn_out.shape[2]
    assert o_cv.shape == o_dn.shape and w_conv_out.shape == w_dn_out.shape
    tn = min(512, d)
    tm = _pick(m, 1040, 16)
    g0 = gate_col0 // tn
    assert gate_col0 % tn == 0 and d % tn == 0
    x_spec = pl.BlockSpec((tm, k), lambda i, j: (i, 0))
    w_spec = pl.BlockSpec((1, k, tn), lambda i, j: (layer, 0, j))
    vmem = 2 * (2 * tm * k * 2 + 2 * k * tn * 4 + 2 * tm * tn * 4 + tm * tn * 2) + 2 * k * tn * 2 + 4 * tm * tn * 4
    return pl.pallas_call(
        _branch_kernel,
        grid=(m // tm, d // tn),
        in_specs=[x_spec, x_spec, w_spec, w_spec,
                  pl.BlockSpec((tm, tn), lambda i, j: (i, g0 + j)),
                  pl.BlockSpec((tm, tn), lambda i, j: (i, g0 + d // tn + j))],
        out_specs=pl.BlockSpec((tm, tn), lambda i, j: (i, j)),
        out_shape=jax.ShapeDtypeStruct((m, d), BF16),
        compiler_params=_params(("parallel", "arbitrary"), vmem + (4 << 20)),
        name="branches",
    )(o_dn, o_cv, w_dn_out, w_conv_out, tail, tail)


def _ffn_kernel(x_ref, w1_ref, w3_ref, w2_ref, o_ref):
    @pl.when(pl.program_id(1) == 0)
    def _():
        o_ref[...] = jnp.zeros_like(o_ref)

    x = x_ref[...]
    h1 = jnp.dot(x, w1_ref[0].astype(BF16), preferred_element_type=F32)
    h3 = jnp.dot(x, w3_ref[0].astype(BF16), preferred_element_type=F32)
    act = (_silu(h1) * h3).astype(BF16)
    o_ref[...] += jnp.dot(act, w2_ref[0].astype(BF16), preferred_element_type=F32)


def _ffn(x, w1, w3, w2, layer):
    m, d = x.shape
    ff = w1.shape[2]
    tm = _pick(m, 1040, 16)
    tf = _pick(ff, 256, LANES)
    vmem = 2 * (tm * d * 2 + 3 * d * tf * 4 + tm * d * 4) + 3 * d * tf * 2 + 4 * tm * tf * 4 + tm * d * 4
    return pl.pallas_call(
        _ffn_kernel,
        grid=(m // tm, ff // tf),
        in_specs=[pl.BlockSpec((tm, d), lambda i, f: (i, 0)),
                  pl.BlockSpec((1, d, tf), lambda i, f: (layer, 0, f)),
                  pl.BlockSpec((1, d, tf), lambda i, f: (layer, 0, f)),
                  pl.BlockSpec((1, tf, d), lambda i, f: (layer, f, 0))],
        out_specs=pl.BlockSpec((tm, d), lambda i, f: (i, 0)),
        out_shape=jax.ShapeDtypeStruct((m, d), F32),
        compiler_params=_params(("parallel", "arbitrary"), vmem + (4 << 20)),
        name="ffn",
    )(x, w1, w3, w2)


def _router_kernel(h_ref, d_ref, nw_ref, rw_ref, wt_ref, ix_ref, cnt_ref, *, n_exp):
    @pl.when(pl.program_id(0) == 0)
    def _():
        cnt_ref[...] = jnp.zeros_like(cnt_ref)

    h = h_ref[...] + d_ref[...]
    tr = h.shape[0]
    u = h * lax.rsqrt(jnp.mean(h * h, axis=-1, keepdims=True) + EPS) * nw_ref[...]
    logits = jnp.dot(u, rw_ref[...], precision=HI, preferred_element_type=F32)
    lane = lax.broadcasted_iota(jnp.int32, logits.shape, 1)
    logits = jnp.where(lane < n_exp, logits, NEG_BIG)
    m1 = jnp.max(logits, axis=-1, keepdims=True)
    i1 = jnp.min(jnp.where(logits == m1, lane, LANES), axis=-1, keepdims=True)
    rest = jnp.where(lane == i1, NEG_BIG, logits)
    m2 = jnp.max(rest, axis=-1, keepdims=True)
    i2 = jnp.min(jnp.where(rest == m2, lane, LANES), axis=-1, keepdims=True)
    e2 = jnp.exp(m2 - m1)
    den = 1.0 + e2
    wt_ref[...] = jnp.where(lane == 0, 1.0 / den, jnp.where(lane == 1, e2 / den, 0.0))

    chosen = jnp.where((lane == i1) | (lane == i2), 1.0, 0.0)
    ri = lax.broadcasted_iota(jnp.int32, (tr, tr), 0)
    ci = lax.broadcasted_iota(jnp.int32, (tr, tr), 1)
    before = jnp.where(ri > ci, 1.0, 0.0).astype(BF16)
    prefix = jnp.dot(before, chosen.astype(BF16), preferred_element_type=F32) + cnt_ref[0:1, :]
    ra = jnp.sum(jnp.where(lane == i1, prefix, 0.0), axis=-1, keepdims=True).astype(jnp.int32)
    rb = jnp.sum(jnp.where(lane == i2, prefix, 0.0), axis=-1, keepdims=True).astype(jnp.int32)
    ix_ref[...] = jnp.where(lane == 0, i1, jnp.where(lane == 1, i2, jnp.where(lane == 2, ra, jnp.where(lane == 3, rb, 0))))
    cnt_ref[...] = cnt_ref[...] + jnp.sum(chosen, axis=0, keepdims=True)


def _router(h, delta, norm_w, router_w):
    m, d = h.shape
    n_exp = router_w.shape[1]
    assert 2 <= n_exp <= LANES
    rw = jnp.pad(router_w, ((0, 0), (0, LANES - n_exp)))
    tr = _pick(m, 520, 8)
    row_spec = pl.BlockSpec((tr, d), lambda i: (i, 0))
    lane_spec = pl.BlockSpec((tr, LANES), lambda i: (i, 0))
    return pl.pallas_call(
        functools.partial(_router_kernel, n_exp=n_exp),
        grid=(m // tr,),
        in_specs=[row_spec, row_spec, pl.BlockSpec((1, d), lambda i: (0, 0)), pl.BlockSpec((d, LANES), lambda i: (0, 0))],
        out_specs=[lane_spec, lane_spec, pl.BlockSpec((8, LANES), lambda i: (0, 0))],
        out_shape=[jax.ShapeDtypeStruct((m, LANES), F32), jax.ShapeDtypeStruct((m, LANES), jnp.int32),
                   jax.ShapeDtypeStruct((8, LANES), F32)],
        compiler_params=_params(("arbitrary",), 12 * tr * d * 4),
        name="router",
    )(h, delta, norm_w.reshape(1, d), rw)


MOE_TILE = 1152
MOE_SUBS = 3


def _moe_tile_rows(m):
    return MOE_TILE if m >= 4 * MOE_TILE else 96


def _dispatch_kernel(pa_ref, pb_ref, u_ref, z_ref, xs_ref, sem):
    del z_ref
    i = pl.program_id(0)
    tr = u_ref.shape[0]

    def copy(r, pos_ref):
        return pltpu.make_async_copy(u_ref.at[r], xs_ref.at[pos_ref[i * tr + r]], sem)

    def issue(r, carry):
        copy(r, pa_ref).start()
        copy(r, pb_ref).start()
        return carry

    def drain(r, carry):
        copy(r, pa_ref).wait()
        copy(r, pb_ref).wait()
        return carry

    lax.fori_loop(0, tr, issue, 0)
    lax.fori_loop(0, tr, drain, 0)


def _dispatch(u3, pos_a, pos_b, n_rows):
    m, s, l = u3.shape
    tr = _pick(m, 520, 8)
    return pl.pallas_call(
        _dispatch_kernel,
        grid_spec=pltpu.PrefetchScalarGridSpec(
            num_scalar_prefetch=2, grid=(m // tr,),
            in_specs=[pl.BlockSpec((tr, s, l), lambda i, pa, pb: (i, 0, 0)), pl.BlockSpec(memory_space=pl.ANY)],
            out_specs=pl.BlockSpec(memory_space=pl.ANY),
            scratch_shapes=[pltpu.SemaphoreType.DMA]),
        out_shape=jax.ShapeDtypeStruct((n_rows, s, l), u3.dtype),
        input_output_aliases={3: 0},
        compiler_params=_params(("arbitrary",), 4 * tr * s * l * 2),
        name="moe_dispatch",
    )(pos_a, pos_b, u3, jnp.zeros((n_rows, s, l), u3.dtype))


def _moe_ffn_kernel(te_ref, rows_ref, src_ref, x_ref, w1_ref, w3_ref, w2_ref, o_ref, w1b_ref, w3b_ref, w2b_ref):
    del te_ref, src_ref
    s = pl.program_id(0)
    f = pl.program_id(1)
    rows = rows_ref[s]
    sub = x_ref.shape[0] // MOE_SUBS

    def block(i):
        sl = slice(i * sub, (i + 1) * sub)
        x = x_ref[sl, :]
        h1 = jnp.dot(x, w1b_ref[...], preferred_element_type=F32)
        h3 = jnp.dot(x, w3b_ref[...], preferred_element_type=F32)
        act = (_silu(h1) * h3).astype(BF16)
        o_ref[sl, :] += jnp.dot(act, w2b_ref[...], preferred_element_type=F32)

    @pl.when(f == 0)
    def _():
        o_ref[...] = jnp.zeros_like(o_ref)

    @pl.when(rows > 0)
    def _():
        w1b_ref[...] = w1_ref[0].astype(BF16)
        w3b_ref[...] = w3_ref[0].astype(BF16)
        w2b_ref[...] = w2_ref[0].astype(BF16)
        block(0)

    for i in range(1, MOE_SUBS):
        @pl.when(rows > i * sub)
        def _():
            block(i)


def _moe_ffn(xs, w1, w3, w2, tile_expert, tile_rows, tile_src, r):
    p, d = xs.shape
    ff = w1.shape[2]
    tf = _pick(ff, 256, LANES)
    nf = ff // tf
    f_of = lambda s, f, rows: jnp.where(rows[s] > 0, f, nf - 1)
    vmem = 2 * (r * d * 2 + 3 * d * tf * 4 + r * d * 4) + 3 * d * tf * 2 + 6 * (r // MOE_SUBS) * tf * 4
    return pl.pallas_call(
        _moe_ffn_kernel,
        grid_spec=pltpu.PrefetchScalarGridSpec(
            num_scalar_prefetch=3, grid=(p // r, nf),
            in_specs=[pl.BlockSpec((r, d), lambda s, f, te, rows, src: (src[s], 0)),
                      pl.BlockSpec((1, d, tf), lambda s, f, te, rows, src: (te[s], 0, f_of(s, f, rows))),
                      pl.BlockSpec((1, d, tf), lambda s, f, te, rows, src: (te[s], 0, f_of(s, f, rows))),
                      pl.BlockSpec((1, tf, d), lambda s, f, te, rows, src: (te[s], f_of(s, f, rows), 0))],
            out_specs=pl.BlockSpec((r, d), lambda s, f, te, rows, src: (s, 0)),
            scratch_shapes=[pltpu.VMEM((d, tf), BF16), pltpu.VMEM((d, tf), BF16), pltpu.VMEM((tf, d), BF16)]),
        out_shape=jax.ShapeDtypeStruct((p, d), F32),
        compiler_params=_params(("arbitrary", "arbitrary"), vmem + (4 << 20)),
        name="moe_ffn",
    )(tile_expert, tile_rows, tile_src, xs, w1, w3, w2)


def _combine_kernel(pa_ref, pb_ref, wt_ref, ys_ref, o_ref, bufa_ref, bufb_ref, sem):
    i = pl.program_id(0)
    tr = o_ref.shape[0]

    def copy(r, pos_ref, buf_ref):
        return pltpu.make_async_copy(ys_ref.at[pl.ds(pos_ref[i * tr + r], 1), :], buf_ref.at[pl.ds(r, 1), :], sem)

    def issue(r, carry):
        copy(r, pa_ref, bufa_ref).start()
        copy(r, pb_ref, bufb_ref).start()
        return carry

    def drain(r, carry):
        copy(r, pa_ref, bufa_ref).wait()
        copy(r, pb_ref, bufb_ref).wait()
        return carry

    lax.fori_loop(0, tr, issue, 0)
    lax.fori_loop(0, tr, drain, 0)
    o_ref[...] = wt_ref[:, 0:1] * bufa_ref[...] + wt_ref[:, 1:2] * bufb_ref[...]


def _combine(ys, wt, pos_a, pos_b):
    m = wt.shape[0]
    d = ys.shape[1]
    tr = _pick(m, 520, 8)
    return pl.pallas_call(
        _combine_kernel,
        grid_spec=pltpu.PrefetchScalarGridSpec(
            num_scalar_prefetch=2, grid=(m // tr,),
            in_specs=[pl.BlockSpec((tr, LANES), lambda i, pa, pb: (i, 0)), pl.BlockSpec(memory_space=pl.ANY)],
            out_specs=pl.BlockSpec((tr, d), lambda i, pa, pb: (i, 0)),
            scratch_shapes=[pltpu.VMEM((tr, d), F32), pltpu.VMEM((tr, d), F32), pltpu.SemaphoreType.DMA]),
        out_shape=jax.ShapeDtypeStruct((m, d), F32),
        compiler_params=_params(("arbitrary",), 8 * tr * d * 4),
        name="moe_combine",
    )(pos_a, pos_b, wt, ys)


def _moe(h, delta, u, norm_w, router_w, w1, w3, w2, e0):
    m, d = u.shape
    n_exp = router_w.shape[1]
    r = _moe_tile_rows(m)
    n_tiles = (2 * m + n_exp * (r - 1) + r - 1) // r
    wt, ix, cnt = _router(h, delta, norm_w, router_w)

    cnt = cnt[0, :n_exp].astype(jnp.int32)
    nt = (cnt + r - 1) // r
    tend = jnp.cumsum(nt)
    tstart = tend - nt
    pos_a = tstart[ix[:, 0]] * r + ix[:, 2]
    pos_b = tstart[ix[:, 1]] * r + ix[:, 3]
    n_used = tend[n_exp - 1]
    s = jnp.arange(n_tiles, dtype=jnp.int32)
    tile_src = jnp.minimum(s, n_used - 1)
    tile_expert = jnp.minimum(jnp.searchsorted(tend, tile_src, side="right"), n_exp - 1).astype(jnp.int32)
    tile_rows = jnp.where(s < n_used, jnp.clip(cnt[tile_expert] - (s - tstart[tile_expert]) * r, 0, r), 0).astype(jnp.int32)

    assert d % LANES == 0
    xs = _dispatch(u.reshape(m, d // LANES, LANES), pos_a, pos_b, n_tiles * r)
    ys = _moe_ffn(xs.reshape(n_tiles * r, d), w1, w3, w2, tile_expert + e0, tile_rows, tile_src, r)
    return _combine(ys, wt, pos_a, pos_b)


def kernel(x, meta_tokens, attn_norm, w_in, short_conv_w, a_log, dt_bias, dn_norm, w_dn_out, dw_conv_w, dw_conv_b,
           conv_ln_g, conv_ln_b, w_conv_out, w_merge_out, ffn_norm, dense_w1, dense_w3, dense_w2, router_w,
           moe_w1, moe_w3, moe_w2, final_norm):
    bsz, seq, d = x.shape
    n_meta = meta_tokens.shape[0]
    depth, heads = a_log.shape
    pad = (-n_meta) % CHUNK
    lp = pad + n_meta + seq
    m = bsz * lp
    hd = heads * DN_DK
    ch = dw_conv_w.shape[2]
    assert lp % CHUNK == 0 and DN_DK == DN_DV == LANES and heads <= LANES
    off_b = 3 * hd + heads * DN_DV
    off_glu = off_b + 2 * heads
    assert w_in.shape[2] == off_glu + 2 * ch + 2 * d

    meta = jnp.broadcast_to(meta_tokens[None].astype(x.dtype), (bsz, n_meta, d))
    h = jnp.concatenate([jnp.zeros((bsz, pad, d), x.dtype), meta, x], axis=1).reshape(m, d)

    delta = None
    for i in range(depth):
        h_new, u = _addnorm(h, delta, attn_norm[i], lp=lp, pad=pad, want_h=delta is not None)
        h = h if delta is None else h_new
        qkvz = _mm(u, w_in, i, n_out=off_b, name="in_proj_qkvz")
        tail = _mm(u, w_in[i:i + 1, :, off_glu:], 0, name="in_proj_tail")
        w_ba = jnp.zeros((1, d, 2 * LANES), F32)
        w_ba = w_ba.at[0, :, 0:heads].set(w_in[i, :, off_b:off_b + heads])
        w_ba = w_ba.at[0, :, LANES:LANES + heads].set(w_in[i, :, off_b + heads:off_glu])
        ba = _mm(u, w_ba, 0, name="in_proj_ba")
        gate_params = jnp.zeros((8, LANES), F32)
        gate_params = gate_params.at[0, 0:heads].set(a_log[i]).at[1, 0:heads].set(dt_bias[i])

        qkvz3 = qkvz.reshape(bsz, lp, off_b)
        tail3 = tail.reshape(bsz, lp, 2 * ch + 2 * d)
        uu, ww, qd, kd, qk, gc = _gdn_prep(qkvz3, ba.reshape(bsz, lp, 2 * LANES), short_conv_w[i], gate_params,
                                            heads=heads, pad=pad)
        o_dn = _gdn_scan(uu, ww, qd, kd, qk, gc, qkvz3, dn_norm[i], heads=heads)
        o_cv = _conformer(tail3, dw_conv_w[i], dw_conv_b[i], conv_ln_g[i], conv_ln_b[i])
        merged = _branches(o_dn.reshape(m, hd), o_cv.reshape(m, ch), w_dn_out, w_conv_out, i, tail, 2 * ch)
        delta = _mm(merged, w_merge_out, i, name="merge_out")

        j = i // 2
        h_prev = h
        h, u = _addnorm(h, delta, ffn_norm[i], lp=lp, pad=0, want_h=True)
        if i % 2 == 0:
            delta = _ffn(u, dense_w1, dense_w3, dense_w2, j)
        else:
            n_exp = moe_w1.shape[1]
            flat = lambda t: t.reshape((-1,) + t.shape[2:])
            delta = _moe(h_prev, delta, u, ffn_norm[i], router_w[j], flat(moe_w1), flat(moe_w3), flat(moe_w2), j * n_exp)

    return _final_norm(h.reshape(bsz, lp, d), delta.reshape(bsz, lp, d), final_norm, skip=pad + n_meta)
```

```python
import functools

import jax
import jax.numpy as jnp
from jax import lax
from jax.experimental import pallas as pl
from jax.experimental.pallas import tpu as pltpu

DN_DK = 128
DN_DV = 128
CHUNK = 64
EPS = 1e-6
LANES = 128
NEG_BIG = -1e30
V7X_VMEM_BYTES = 64 * 1024 * 1024
VMEM_CAP = 56 * 1024 * 1024

F32 = jnp.float32
BF16 = jnp.bfloat16
HI = lax.Precision.HIGHEST


def _pick(n, target, mult):
    best = None
    for d in range(mult, min(n, target) + 1, mult):
        if n % d == 0:
            best = d
    assert best is not None, (n, target, mult)
    return best


def _params(sem, vmem_bytes):
    return pltpu.CompilerParams(dimension_semantics=sem,
                                vmem_limit_bytes=int(min(VMEM_CAP, max(vmem_bytes, 16 * 1024 * 1024))))


def _sigmoid(x):
    return 0.5 * jnp.tanh(0.5 * x) + 0.5


def _silu(x):
    return x * _sigmoid(x)


def _softplus(x):
    return jnp.maximum(x, 0.0) + jnp.log(1.0 + jnp.exp(-jnp.abs(x)))


def _addnorm_kernel(*refs, lp, pad, has_delta, want_h):
    it = iter(refs)
    h_ref = next(it)
    d_ref = next(it) if has_delta else None
    w_ref = next(it)
    hn_ref = next(it) if want_h else None
    u_ref = next(it)
    tr = h_ref.shape[0]
    h = h_ref[...]
    if has_delta:
        h = h + d_ref[...]
    if want_h:
        hn_ref[...] = h
    y = h * lax.rsqrt(jnp.mean(h * h, axis=-1, keepdims=True) + EPS) * w_ref[...]
    row = pl.program_id(0) * tr + lax.broadcasted_iota(jnp.int32, (tr, 1), 0)
    y = jnp.where(row % lp >= pad, y, 0.0)
    u_ref[...] = y.astype(u_ref.dtype)


def _addnorm(h, delta, w, *, lp, pad, want_h):
    m, d = h.shape
    tr = _pick(m, 520, 8)
    has_delta = delta is not None
    row_spec = pl.BlockSpec((tr, d), lambda i: (i, 0))
    in_specs = [row_spec] + ([row_spec] if has_delta else []) + [pl.BlockSpec((1, d), lambda i: (0, 0))]
    out_shape = ([jax.ShapeDtypeStruct((m, d), F32)] if want_h else []) + [jax.ShapeDtypeStruct((m, d), BF16)]
    out_specs = ([row_spec] if want_h else []) + [row_spec]
    args = [h] + ([delta] if has_delta else []) + [w.reshape(1, d)]
    outs = pl.pallas_call(
        functools.partial(_addnorm_kernel, lp=lp, pad=pad, has_delta=has_delta, want_h=want_h),
        grid=(m // tr,), in_specs=in_specs, out_specs=out_specs, out_shape=out_shape,
        compiler_params=_params(("parallel",), 12 * tr * d * 4),
        name="addnorm",
    )(*args)
    return outs if want_h else (None, outs[0])


def _final_norm_kernel(h_ref, d_ref, w_ref, o_ref):
    h = h_ref[0] + d_ref[0]
    o_ref[0] = h * lax.rsqrt(jnp.mean(h * h, axis=-1, keepdims=True) + EPS) * w_ref[...]


def _final_norm(h, delta, w, *, skip):
    b, lp, d = h.shape
    seq = lp - skip
    tr = skip
    assert seq % tr == 0
    in_spec = pl.BlockSpec((1, tr, d), lambda bi, i: (bi, i + 1, 0))
    return pl.pallas_call(
        _final_norm_kernel,
        grid=(b, seq // tr),
        in_specs=[in_spec, in_spec, pl.BlockSpec((1, d), lambda bi, i: (0, 0))],
        out_specs=pl.BlockSpec((1, tr, d), lambda bi, i: (bi, i, 0)),
        out_shape=jax.ShapeDtypeStruct((b, seq, d), F32),
        compiler_params=_params(("parallel", "parallel"), 12 * tr * d * 4),
        name="final_norm",
    )(h, delta, w.reshape(1, d))


def _mm_kernel(x_ref, w_ref, o_ref):
    o_ref[...] = jnp.dot(x_ref[...], w_ref[0].astype(BF16), preferred_element_type=F32).astype(o_ref.dtype)


def _mm(x, w, layer, *, n_out=None, tn=512, out_dtype=F32, name="mm"):
    m, k = x.shape
    n_out = w.shape[2] if n_out is None else n_out
    tn = min(tn, n_out)
    assert n_out % tn == 0
    tm = _pick(m, 2080, 16)
    vmem = 2 * (tm * k * 2 + k * tn * 4 + tm * tn * 4) + k * tn * 2 + tm * tn * 4
    return pl.pallas_call(
        _mm_kernel,
        grid=(m // tm, n_out // tn),
        in_specs=[pl.BlockSpec((tm, k), lambda i, j: (i, 0)),
                  pl.BlockSpec((1, k, tn), lambda i, j: (layer, 0, j))],
        out_specs=pl.BlockSpec((tm, tn), lambda i, j: (i, j)),
        out_shape=jax.ShapeDtypeStruct((m, n_out), out_dtype),
        compiler_params=_params(("parallel", "arbitrary"), vmem + (4 << 20)),
        name=name,
    )(x, w)


def _gdn_kernel(q_ref, k_ref, v_ref, qh_ref, kh_ref, vh_ref, z_ref, ba_ref, cw_ref, gp_ref, nw_ref,
                o_ref, s_ref, ext_ref, *, heads, pad):
    c = CHUNK
    hd = heads * DN_DK
    n = pl.program_id(1)

    @pl.when(n == 0)
    def _():
        s_ref[...] = jnp.zeros_like(s_ref)

    def conv_silu(main_ref, halo_ref, part):
        ext_ref[0:8, :] = halo_ref[0]
        ext_ref[8:8 + c, :] = main_ref[0]
        kw = cw_ref.shape[0]
        acc = jnp.zeros((c, hd), F32)
        for j in range(kw):
            acc = acc + cw_ref[j:j + 1, part * hd:(part + 1) * hd] * ext_ref[8 - (kw - 1) + j:8 - (kw - 1) + j + c, :]
        return _silu(acc)

    row = n * c + lax.broadcasted_iota(jnp.int32, (c, 1), 0)
    live = row >= pad
    beta = jnp.where(live, _sigmoid(ba_ref[0, :, 0:LANES]), 0.0)
    g = jnp.where(live, -jnp.exp(gp_ref[0:1, :]) * _softplus(ba_ref[0, :, LANES:2 * LANES] + gp_ref[1:2, :]), 0.0)

    ii = lax.broadcasted_iota(jnp.int32, (2 * c, c), 0)
    jj = lax.broadcasted_iota(jnp.int32, (2 * c, c), 1)
    tri_pad = jnp.where((ii >= jj) & (ii < c), 1.0, 0.0).astype(F32)
    gc_pad = jnp.dot(tri_pad, g, precision=HI, preferred_element_type=F32)
    gc = gc_pad[0:c]
    gct = gc_pad.T
    eg = jnp.exp(gc)
    erev = jnp.exp(gc[c - 1:c, :] - gc)
    cd = jnp.exp(gc[c - 1:c, :])

    i1 = lax.broadcasted_iota(jnp.int32, (c, c), 0)
    j1 = lax.broadcasted_iota(jnp.int32, (c, c), 1)
    causal = i1 >= j1
    strict = i1 > j1

    qa = conv_silu(q_ref, qh_ref, 0)
    ka = conv_silu(k_ref, kh_ref, 1)
    va = conv_silu(v_ref, vh_ref, 2)

    def bdot(a, b):
        return jnp.dot(a.astype(BF16), b.astype(BF16), preferred_element_type=F32)

    a_list, rhs_list, qk_list, qd_list, kd_list = [], [], [], [], []
    for h in range(heads):
        sl = slice(h * DN_DK, (h + 1) * DN_DK)
        qh = qa[:, sl]
        kh = ka[:, sl]
        qh = qh * lax.rsqrt(jnp.sum(qh * qh, axis=-1, keepdims=True) + EPS) * (DN_DK ** -0.5)
        kh = kh * lax.rsqrt(jnp.sum(kh * kh, axis=-1, keepdims=True) + EPS)
        bh = beta[:, h:h + 1]
        egh = eg[:, h:h + 1]
        kb = kh * bh
        decay = jnp.exp(jnp.where(causal, gc[:, h:h + 1] - gct[h:h + 1, 0:c], NEG_BIG))
        lhs = jnp.concatenate([kb, qh], axis=0).astype(BF16)
        kq = lax.dot_general(lhs, kh.astype(BF16), (((1,), (1,)), ((), ())), preferred_element_type=F32)
        a_list.append(jnp.where(strict, kq[0:c] * decay, 0.0))
        qk_list.append((kq[c:2 * c] * decay).astype(BF16))
        rhs_list.append(jnp.concatenate([va[:, sl] * bh, kb * egh], axis=1))
        qd_list.append(qh * egh)
        kd_list.append((kh * erev[:, h:h + 1]).astype(BF16))

    m_list = [-a for a in a_list]
    p_list = a_list
    span = 2
    while span < c:
        p_list = [bdot(p, p) for p in p_list]
        m_list = [mm + p + bdot(mm, p) for mm, p in zip(m_list, p_list)]
        span *= 2

    for h in range(heads):
        sl = slice(h * DN_DK, (h + 1) * DN_DK)
        sol = rhs_list[h] + bdot(m_list[h], rhs_list[h])
        s = s_ref[h]
        ws_qs = bdot(jnp.concatenate([sol[:, DN_DV:DN_DV + DN_DK], qd_list[h]], axis=0), s)
        v_new = (sol[:, 0:DN_DV] - ws_qs[0:c]).astype(BF16)
        o = ws_qs[c:2 * c] + jnp.dot(qk_list[h], v_new, preferred_element_type=F32)
        s_ref[h] = s * cd[:, h:h + 1] + lax.dot_general(
            kd_list[h], v_new, (((0,), (0,)), ((), ())), preferred_element_type=F32)
        on = o * lax.rsqrt(jnp.mean(o * o, axis=-1, keepdims=True) + EPS) * nw_ref[...]
        o_ref[0, :, sl] = (on * _silu(z_ref[0, :, sl])).astype(o_ref.dtype)


def _gdn(qkvz, ba, conv_w, gate_params, norm_w, *, heads, pad):
    b, lp, _ = qkvz.shape
    hd = heads * DN_DK
    nc = lp // CHUNK
    main = lambda part: pl.BlockSpec((1, CHUNK, hd), lambda bi, n: (bi, n, part))
    halo = lambda part: pl.BlockSpec((1, 8, hd), lambda bi, n: (bi, jnp.maximum(n * (CHUNK // 8) - 1, 0), part))
    return pl.pallas_call(
        functools.partial(_gdn_kernel, heads=heads, pad=pad),
        grid=(b, nc),
        in_specs=[main(0), main(1), main(2), halo(0), halo(1), halo(2), main(3),
                  pl.BlockSpec((1, CHUNK, 2 * LANES), lambda bi, n: (bi, n, 0)),
                  pl.BlockSpec(conv_w.shape, lambda bi, n: (0, 0)),
                  pl.BlockSpec(gate_params.shape, lambda bi, n: (0, 0)),
                  pl.BlockSpec((1, DN_DV), lambda bi, n: (0, 0))],
        out_specs=pl.BlockSpec((1, CHUNK, hd), lambda bi, n: (bi, n, 0)),
        out_shape=jax.ShapeDtypeStruct((b, lp, hd), BF16),
        scratch_shapes=[pltpu.VMEM((heads, DN_DK, DN_DV), F32), pltpu.VMEM((8 + CHUNK, hd), F32)],
        compiler_params=_params(("parallel", "arbitrary"), 40 * CHUNK * hd * 4),
        name="gdn",
    )(qkvz, qkvz, qkvz, qkvz, qkvz, qkvz, qkvz, ba, conv_w, gate_params, norm_w.reshape(1, DN_DV))


CONV_HALO = 32
CONV_CT = LANES
SUBLANES = 8


def _conformer_kernel(a_ref, g_ref, ah_ref, gh_ref, cw_ref, cb_ref, lg_ref, lb_ref, o_ref, ext_ref, sh_ref, y_ref):
    r, ch = o_ref.shape[1], o_ref.shape[2]
    kw = cw_ref.shape[0]
    ext_ref[0:CONV_HALO, :] = ah_ref[0].astype(F32) * _sigmoid(gh_ref[0].astype(F32))
    ext_ref[CONV_HALO:CONV_HALO + r, :] = a_ref[0].astype(F32) * _sigmoid(g_ref[0].astype(F32))
    base = CONV_HALO - (kw - 1)
    span = sh_ref.shape[1]

    def ch_body(ci, carry):
        c0 = pl.multiple_of(ci * CONV_CT, CONV_CT)
        for s in range(1, SUBLANES):
            sh_ref[s - 1] = ext_ref[s:s + span, pl.ds(c0, CONV_CT)]
        acc = jnp.broadcast_to(cb_ref[:, pl.ds(c0, CONV_CT)], (r, CONV_CT))
        for j in range(kw):
            q, s = divmod(base + j, SUBLANES)
            rows = slice(q * SUBLANES, q * SUBLANES + r)
            tap = sh_ref[s - 1, rows, :] if s else ext_ref[rows, pl.ds(c0, CONV_CT)]
            acc = acc + cw_ref[j:j + 1, pl.ds(c0, CONV_CT)] * tap
        y_ref[:, pl.ds(c0, CONV_CT)] = acc
        return carry

    lax.fori_loop(0, ch // CONV_CT, ch_body, 0)
    y = y_ref[...]
    mu = jnp.mean(y, axis=-1, keepdims=True)
    yc = y - mu
    var = jnp.mean(yc * yc, axis=-1, keepdims=True)
    yn = yc * lax.rsqrt(var + EPS) * lg_ref[...] + lb_ref[...]
    o_ref[0] = _silu(yn).astype(o_ref.dtype)


def _conformer(tail, conv_w, conv_b, ln_g, ln_b):
    b, lp, _ = tail.shape
    ch = conv_w.shape[1]
    assert conv_w.shape[0] - 1 <= CONV_HALO and ch % CONV_CT == 0
    r = _pick(lp, 320, CONV_HALO)
    main = lambda col: pl.BlockSpec((1, r, ch), lambda bi, i: (bi, i, col))
    halo = lambda col: pl.BlockSpec((1, CONV_HALO, ch), lambda bi, i: (bi, jnp.maximum(i * (r // CONV_HALO) - 1, 0), col))
    vec = pl.BlockSpec((1, ch), lambda bi, i: (0, 0))
    return pl.pallas_call(
        _conformer_kernel,
        grid=(b, lp // r),
        in_specs=[main(0), main(1), halo(0), halo(1), pl.BlockSpec(conv_w.shape, lambda bi, i: (0, 0)), vec, vec, vec],
        out_specs=pl.BlockSpec((1, r, ch), lambda bi, i: (bi, i, 0)),
        out_shape=jax.ShapeDtypeStruct((b, lp, ch), BF16),
        scratch_shapes=[pltpu.VMEM((CONV_HALO + r, ch), F32),
                        pltpu.VMEM((SUBLANES - 1, CONV_HALO + r - SUBLANES, CONV_CT), F32),
                        pltpu.VMEM((r, ch), F32)],
        compiler_params=_params(("parallel", "arbitrary"), 14 * r * ch * 4),
        name="conformer",
    )(tail, tail, tail, tail, conv_w, conv_b.reshape(1, ch), ln_g.reshape(1, ch), ln_b.reshape(1, ch))


def _branch_kernel(od_ref, oc_ref, wd_ref, wc_ref, gd_ref, gc_ref, o_ref):
    yd = jnp.dot(od_ref[...], wd_ref[0].astype(BF16), preferred_element_type=F32)
    yc = jnp.dot(oc_ref[...], wc_ref[0].astype(BF16), preferred_element_type=F32)
    o_ref[...] = (_sigmoid(gd_ref[...].astype(F32)) * yd + _sigmoid(gc_ref[...].astype(F32)) * yc).astype(o_ref.dtype)


def _branches(o_dn, o_cv, w_dn_out, w_conv_out, layer, tail, gate_col0):
    m, k = o_dn.shape
    d = w_dn_out.shape[2]
    assert o_cv.shape == o_dn.shape and w_conv_out.shape == w_dn_out.shape
    tn = min(512, d)
    tm = _pick(m, 1040, 16)
    g0 = gate_col0 // tn
    assert gate_col0 % tn == 0 and d % tn == 0
    x_spec = pl.BlockSpec((tm, k), lambda i, j: (i, 0))
    w_spec = pl.BlockSpec((1, k, tn), lambda i, j: (layer, 0, j))
    vmem = 2 * (2 * tm * k * 2 + 2 * k * tn * 4 + 2 * tm * tn * 4 + tm * tn * 2) + 2 * k * tn * 2 + 4 * tm * tn * 4
    return pl.pallas_call(
        _branch_kernel,
        grid=(m // tm, d // tn),
        in_specs=[x_spec, x_spec, w_spec, w_spec,
                  pl.BlockSpec((tm, tn), lambda i, j: (i, g0 + j)),
                  pl.BlockSpec((tm, tn), lambda i, j: (i, g0 + d // tn + j))],
        out_specs=pl.BlockSpec((tm, tn), lambda i, j: (i, j)),
        out_shape=jax.ShapeDtypeStruct((m, d), BF16),
        compiler_params=_params(("parallel", "arbitrary"), vmem + (4 << 20)),
        name="branches",
    )(o_dn, o_cv, w_dn_out, w_conv_out, tail, tail)


def _ffn_kernel(x_ref, w1_ref, w3_ref, w2_ref, o_ref):
    @pl.when(pl.program_id(1) == 0)
    def _():
        o_ref[...] = jnp.zeros_like(o_ref)

    x = x_ref[...]
    h1 = jnp.dot(x, w1_ref[0].astype(BF16), preferred_element_type=F32)
    h3 = jnp.dot(x, w3_ref[0].astype(BF16), preferred_element_type=F32)
    act = (_silu(h1) * h3).astype(BF16)
    o_ref[...] += jnp.dot(act, w2_ref[0].astype(BF16), preferred_element_type=F32)


def _ffn(x, w1, w3, w2, layer):
    m, d = x.shape
    ff = w1.shape[2]
    tm = _pick(m, 1040, 16)
    tf = _pick(ff, 256, LANES)
    vmem = 2 * (tm * d * 2 + 3 * d * tf * 4 + tm * d * 4) + 3 * d * tf * 2 + 4 * tm * tf * 4 + tm * d * 4
    return pl.pallas_call(
        _ffn_kernel,
        grid=(m // tm, ff // tf),
        in_specs=[pl.BlockSpec((tm, d), lambda i, f: (i, 0)),
                  pl.BlockSpec((1, d, tf), lambda i, f: (layer, 0, f)),
                  pl.BlockSpec((1, d, tf), lambda i, f: (layer, 0, f)),
                  pl.BlockSpec((1, tf, d), lambda i, f: (layer, f, 0))],
        out_specs=pl.BlockSpec((tm, d), lambda i, f: (i, 0)),
        out_shape=jax.ShapeDtypeStruct((m, d), F32),
        compiler_params=_params(("parallel", "arbitrary"), vmem + (4 << 20)),
        name="ffn",
    )(x, w1, w3, w2)


def _router_kernel(h_ref, d_ref, nw_ref, rw_ref, wt_ref, ix_ref, cnt_ref, *, n_exp):
    @pl.when(pl.program_id(0) == 0)
    def _():
        cnt_ref[...] = jnp.zeros_like(cnt_ref)

    h = h_ref[...] + d_ref[...]
    tr = h.shape[0]
    u = h * lax.rsqrt(jnp.mean(h * h, axis=-1, keepdims=True) + EPS) * nw_ref[...]
    logits = jnp.dot(u, rw_ref[...], precision=HI, preferred_element_type=F32)
    lane = lax.broadcasted_iota(jnp.int32, logits.shape, 1)
    logits = jnp.where(lane < n_exp, logits, NEG_BIG)
    m1 = jnp.max(logits, axis=-1, keepdims=True)
    i1 = jnp.min(jnp.where(logits == m1, lane, LANES), axis=-1, keepdims=True)
    rest = jnp.where(lane == i1, NEG_BIG, logits)
    m2 = jnp.max(rest, axis=-1, keepdims=True)
    i2 = jnp.min(jnp.where(rest == m2, lane, LANES), axis=-1, keepdims=True)
    e2 = jnp.exp(m2 - m1)
    den = 1.0 + e2
    wt_ref[...] = jnp.where(lane == 0, 1.0 / den, jnp.where(lane == 1, e2 / den, 0.0))

    chosen = jnp.where((lane == i1) | (lane == i2), 1.0, 0.0)
    ri = lax.broadcasted_iota(jnp.int32, (tr, tr), 0)
    ci = lax.broadcasted_iota(jnp.int32, (tr, tr), 1)
    before = jnp.where(ri > ci, 1.0, 0.0).astype(BF16)
    prefix = jnp.dot(before, chosen.astype(BF16), preferred_element_type=F32) + cnt_ref[0:1, :]
    ra = jnp.sum(jnp.where(lane == i1, prefix, 0.0), axis=-1, keepdims=True).astype(jnp.int32)
    rb = jnp.sum(jnp.where(lane == i2, prefix, 0.0), axis=-1, keepdims=True).astype(jnp.int32)
    ix_ref[...] = jnp.where(lane == 0, i1, jnp.where(lane == 1, i2, jnp.where(lane == 2, ra, jnp.where(lane == 3, rb, 0))))
    cnt_ref[...] = cnt_ref[...] + jnp.sum(chosen, axis=0, keepdims=True)


def _router(h, delta, norm_w, router_w):
    m, d = h.shape
    n_exp = router_w.shape[1]
    assert 2 <= n_exp <= LANES
    rw = jnp.pad(router_w, ((0, 0), (0, LANES - n_exp)))
    tr = _pick(m, 520, 8)
    row_spec = pl.BlockSpec((tr, d), lambda i: (i, 0))
    lane_spec = pl.BlockSpec((tr, LANES), lambda i: (i, 0))
    return pl.pallas_call(
        functools.partial(_router_kernel, n_exp=n_exp),
        grid=(m // tr,),
        in_specs=[row_spec, row_spec, pl.BlockSpec((1, d), lambda i: (0, 0)), pl.BlockSpec((d, LANES), lambda i: (0, 0))],
        out_specs=[lane_spec, lane_spec, pl.BlockSpec((8, LANES), lambda i: (0, 0))],
        out_shape=[jax.ShapeDtypeStruct((m, LANES), F32), jax.ShapeDtypeStruct((m, LANES), jnp.int32),
                   jax.ShapeDtypeStruct((8, LANES), F32)],
        compiler_params=_params(("arbitrary",), 12 * tr * d * 4),
        name="router",
    )(h, delta, norm_w.reshape(1, d), rw)


MOE_TILE = 1152
MOE_SUBS = 3


def _moe_tile_rows(m):
    return MOE_TILE if m >= 4 * MOE_TILE else 96


def _dispatch_kernel(pa_ref, pb_ref, u_ref, z_ref, xs_ref, sem):
    del z_ref
    i = pl.program_id(0)
    tr = u_ref.shape[0]

    def copy(r, pos_ref):
        return pltpu.make_async_copy(u_ref.at[r], xs_ref.at[pos_ref[i * tr + r]], sem)

    def issue(r, carry):
        copy(r, pa_ref).start()
        copy(r, pb_ref).start()
        return carry

    lax.fori_loop(0, tr, issue, 0)
    for _ in range(2):
        pltpu.make_async_copy(u_ref, xs_ref.at[pl.ds(0, tr)], sem).wait()


def _dispatch(u3, pos_a, pos_b, n_rows):
    m, s, l = u3.shape
    tr = _pick(m, 520, 8)
    return pl.pallas_call(
        _dispatch_kernel,
        grid_spec=pltpu.PrefetchScalarGridSpec(
            num_scalar_prefetch=2, grid=(m // tr,),
            in_specs=[pl.BlockSpec((tr, s, l), lambda i, pa, pb: (i, 0, 0)), pl.BlockSpec(memory_space=pl.ANY)],
            out_specs=pl.BlockSpec(memory_space=pl.ANY),
            scratch_shapes=[pltpu.SemaphoreType.DMA]),
        out_shape=jax.ShapeDtypeStruct((n_rows, s, l), u3.dtype),
        input_output_aliases={3: 0},
        compiler_params=_params(("arbitrary",), 4 * tr * s * l * 2),
        name="moe_dispatch",
    )(pos_a, pos_b, u3, jnp.zeros((n_rows, s, l), u3.dtype))


def _moe_ffn_kernel(te_ref, rows_ref, src_ref, x_ref, w1_ref, w3_ref, w2_ref, o_ref, w1b_ref, w3b_ref, w2b_ref):
    del te_ref, src_ref
    s = pl.program_id(0)
    f = pl.program_id(1)
    rows = rows_ref[s]
    sub = x_ref.shape[0] // MOE_SUBS

    def block(i):
        sl = slice(i * sub, (i + 1) * sub)
        x = x_ref[sl, :]
        h1 = jnp.dot(x, w1b_ref[...], preferred_element_type=F32)
        h3 = jnp.dot(x, w3b_ref[...], preferred_element_type=F32)
        act = (_silu(h1) * h3).astype(BF16)
        o_ref[sl, :] += jnp.dot(act, w2b_ref[...], preferred_element_type=F32)

    @pl.when(f == 0)
    def _():
        o_ref[...] = jnp.zeros_like(o_ref)

    @pl.when(rows > 0)
    def _():
        w1b_ref[...] = w1_ref[0].astype(BF16)
        w3b_ref[...] = w3_ref[0].astype(BF16)
        w2b_ref[...] = w2_ref[0].astype(BF16)
        block(0)

    for i in range(1, MOE_SUBS):
        @pl.when(rows > i * sub)
        def _():
            block(i)


def _moe_ffn(xs, w1, w3, w2, tile_expert, tile_rows, tile_src, r):
    p, d = xs.shape
    ff = w1.shape[2]
    tf = _pick(ff, 256, LANES)
    nf = ff // tf
    f_of = lambda s, f, rows: jnp.where(rows[s] > 0, f, nf - 1)
    vmem = 2 * (r * d * 2 + 3 * d * tf * 4 + r * d * 4) + 3 * d * tf * 2 + 6 * (r // MOE_SUBS) * tf * 4
    return pl.pallas_call(
        _moe_ffn_kernel,
        grid_spec=pltpu.PrefetchScalarGridSpec(
            num_scalar_prefetch=3, grid=(p // r, nf),
            in_specs=[pl.BlockSpec((r, d), lambda s, f, te, rows, src: (src[s], 0)),
                      pl.BlockSpec((1, d, tf), lambda s, f, te, rows, src: (te[s], 0, f_of(s, f, rows))),
                      pl.BlockSpec((1, d, tf), lambda s, f, te, rows, src: (te[s], 0, f_of(s, f, rows))),
                      pl.BlockSpec((1, tf, d), lambda s, f, te, rows, src: (te[s], f_of(s, f, rows), 0))],
            out_specs=pl.BlockSpec((r, d), lambda s, f, te, rows, src: (s, 0)),
            scratch_shapes=[pltpu.VMEM((d, tf), BF16), pltpu.VMEM((d, tf), BF16), pltpu.VMEM((tf, d), BF16)]),
        out_shape=jax.ShapeDtypeStruct((p, d), F32),
        compiler_params=_params(("arbitrary", "arbitrary"), vmem + (4 << 20)),
        name="moe_ffn",
    )(tile_expert, tile_rows, tile_src, xs, w1, w3, w2)


def _combine_kernel(pa_ref, pb_ref, wt_ref, ys_ref, o_ref, bufa_ref, bufb_ref, sem):
    i = pl.program_id(0)
    tr = o_ref.shape[0]

    def copy(r, pos_ref, buf_ref):
        return pltpu.make_async_copy(ys_ref.at[pl.ds(pos_ref[i * tr + r], 1), :], buf_ref.at[pl.ds(r, 1), :], sem)

    def issue(r, carry):
        copy(r, pa_ref, bufa_ref).start()
        copy(r, pb_ref, bufb_ref).start()
        return carry

    lax.fori_loop(0, tr, issue, 0)
    pltpu.make_async_copy(ys_ref.at[pl.ds(0, tr), :], bufa_ref, sem).wait()
    pltpu.make_async_copy(ys_ref.at[pl.ds(0, tr), :], bufb_ref, sem).wait()
    o_ref[...] = wt_ref[:, 0:1] * bufa_ref[...] + wt_ref[:, 1:2] * bufb_ref[...]


def _combine(ys, wt, pos_a, pos_b):
    m = wt.shape[0]
    d = ys.shape[1]
    tr = _pick(m, 520, 8)
    return pl.pallas_call(
        _combine_kernel,
        grid_spec=pltpu.PrefetchScalarGridSpec(
            num_scalar_prefetch=2, grid=(m // tr,),
            in_specs=[pl.BlockSpec((tr, LANES), lambda i, pa, pb: (i, 0)), pl.BlockSpec(memory_space=pl.ANY)],
            out_specs=pl.BlockSpec((tr, d), lambda i, pa, pb: (i, 0)),
            scratch_shapes=[pltpu.VMEM((tr, d), F32), pltpu.VMEM((tr, d), F32), pltpu.SemaphoreType.DMA]),
        out_shape=jax.ShapeDtypeStruct((m, d), F32),
        compiler_params=_params(("arbitrary",), 8 * tr * d * 4),
        name="moe_combine",
    )(pos_a, pos_b, wt, ys)


def _moe(h, delta, u, norm_w, router_w, w1, w3, w2, e0):
    m, d = u.shape
    n_exp = router_w.shape[1]
    r = _moe_tile_rows(m)
    n_tiles = (2 * m + n_exp * (r - 1) + r - 1) // r
    wt, ix, cnt = _router(h, delta, norm_w, router_w)

    cnt = cnt[0, :n_exp].astype(jnp.int32)
    nt = (cnt + r - 1) // r
    tend = jnp.cumsum(nt)
    tstart = tend - nt
    pos_a = tstart[ix[:, 0]] * r + ix[:, 2]
    pos_b = tstart[ix[:, 1]] * r + ix[:, 3]
    n_used = tend[n_exp - 1]
    s = jnp.arange(n_tiles, dtype=jnp.int32)
    tile_src = jnp.minimum(s, n_used - 1)
    tile_expert = jnp.minimum(jnp.searchsorted(tend, tile_src, side="right"), n_exp - 1).astype(jnp.int32)
    tile_rows = jnp.where(s < n_used, jnp.clip(cnt[tile_expert] - (s - tstart[tile_expert]) * r, 0, r), 0).astype(jnp.int32)

    assert d % LANES == 0
    xs = _dispatch(u.reshape(m, d // LANES, LANES), pos_a, pos_b, n_tiles * r)
    ys = _moe_ffn(xs.reshape(n_tiles * r, d), w1, w3, w2, tile_expert + e0, tile_rows, tile_src, r)
    return _combine(ys, wt, pos_a, pos_b)


def kernel(x, meta_tokens, attn_norm, w_in, short_conv_w, a_log, dt_bias, dn_norm, w_dn_out, dw_conv_w, dw_conv_b,
           conv_ln_g, conv_ln_b, w_conv_out, w_merge_out, ffn_norm, dense_w1, dense_w3, dense_w2, router_w,
           moe_w1, moe_w3, moe_w2, final_norm):
    bsz, seq, d = x.shape
    n_meta = meta_tokens.shape[0]
    depth, heads = a_log.shape
    pad = (-n_meta) % CHUNK
    lp = pad + n_meta + seq
    m = bsz * lp
    hd = heads * DN_DK
    ch = dw_conv_w.shape[2]
    assert lp % CHUNK == 0 and DN_DK == DN_DV == LANES and heads <= LANES
    off_b = 3 * hd + heads * DN_DV
    off_glu = off_b + 2 * heads
    assert w_in.shape[2] == off_glu + 2 * ch + 2 * d

    meta = jnp.broadcast_to(meta_tokens[None].astype(x.dtype), (bsz, n_meta, d))
    h = jnp.concatenate([jnp.zeros((bsz, pad, d), x.dtype), meta, x], axis=1).reshape(m, d)

    delta = None
    for i in range(depth):
        h_new, u = _addnorm(h, delta, attn_norm[i], lp=lp, pad=pad, want_h=delta is not None)
        h = h if delta is None else h_new
        qkvz = _mm(u, w_in, i, n_out=off_b, name="in_proj_qkvz")
        w_tail = w_in[i:i + 1, :, off_glu:].astype(BF16)
        tail = _mm(u, w_tail, 0, out_dtype=BF16, name="in_proj_tail")
        w_ba = jnp.zeros((1, d, 2 * LANES), F32)
        w_ba = w_ba.at[0, :, 0:heads].set(w_in[i, :, off_b:off_b + heads])
        w_ba = w_ba.at[0, :, LANES:LANES + heads].set(w_in[i, :, off_b + heads:off_glu])
        ba = _mm(u, w_ba, 0, name="in_proj_ba")
        gate_params = jnp.zeros((8, LANES), F32)
        gate_params = gate_params.at[0, 0:heads].set(a_log[i]).at[1, 0:heads].set(dt_bias[i])

        qkvz3 = qkvz.reshape(bsz, lp, off_b)
        tail3 = tail.reshape(bsz, lp, 2 * ch + 2 * d)
        o_dn = _gdn(qkvz3, ba.reshape(bsz, lp, 2 * LANES), short_conv_w[i], gate_params, dn_norm[i], heads=heads, pad=pad)
        o_cv = _conformer(tail3, dw_conv_w[i], dw_conv_b[i], conv_ln_g[i], conv_ln_b[i])
        merged = _branches(o_dn.reshape(m, hd), o_cv.reshape(m, ch), w_dn_out, w_conv_out, i, tail, 2 * ch)
        delta = _mm(merged, w_merge_out, i, name="merge_out")

        j = i // 2
        h_prev = h
        h, u = _addnorm(h, delta, ffn_norm[i], lp=lp, pad=0, want_h=True)
        if i % 2 == 0:
            delta = _ffn(u, dense_w1, dense_w3, dense_w2, j)
        else:
            n_exp = moe_w1.shape[1]
            flat = lambda t: t.reshape((-1,) + t.shape[2:])
            delta = _moe(h_prev, delta, u, ffn_norm[i], router_w[j], flat(moe_w1), flat(moe_w3), flat(moe_w2), j * n_exp)

    return _final_norm(h.reshape(bsz, lp, d), delta.reshape(bsz, lp, d), final_norm, skip=pad + n_meta)
```

```python
import functools

import jax
import jax.numpy as jnp
from jax import lax
from jax.experimental import pallas as pl
from jax.experimental.pallas import tpu as pltpu

DN_DK = 128
DN_DV = 128
CHUNK = 64
EPS = 1e-6
LANES = 128
NEG_BIG = -1e30
V7X_VMEM_BYTES = 64 * 1024 * 1024
VMEM_CAP = 56 * 1024 * 1024

F32 = jnp.float32
BF16 = jnp.bfloat16
HI = lax.Precision.HIGHEST


def _pick(n, target, mult):
    best = None
    for d in range(mult, min(n, target) + 1, mult):
        if n % d == 0:
            best = d
    assert best is not None, (n, target, mult)
    return best


def _params(sem, vmem_bytes):
    return pltpu.CompilerParams(dimension_semantics=sem,
                                vmem_limit_bytes=int(min(VMEM_CAP, max(vmem_bytes, 16 * 1024 * 1024))))


def _sigmoid(x):
    return 0.5 * jnp.tanh(0.5 * x) + 0.5


def _silu(x):
    return x * _sigmoid(x)


def _softplus(x):
    return jnp.maximum(x, 0.0) + jnp.log(1.0 + jnp.exp(-jnp.abs(x)))


def _addnorm_kernel(*refs, lp, pad, has_delta, want_h):
    it = iter(refs)
    h_ref = next(it)
    d_ref = next(it) if has_delta else None
    w_ref = next(it)
    hn_ref = next(it) if want_h else None
    u_ref = next(it)
    tr = h_ref.shape[0]
    h = h_ref[...]
    if has_delta:
        h = h + d_ref[...]
    if want_h:
        hn_ref[...] = h
    y = h * lax.rsqrt(jnp.mean(h * h, axis=-1, keepdims=True) + EPS) * w_ref[...]
    row = pl.program_id(0) * tr + lax.broadcasted_iota(jnp.int32, (tr, 1), 0)
    y = jnp.where(row % lp >= pad, y, 0.0)
    u_ref[...] = y.astype(u_ref.dtype)


def _addnorm(h, delta, w, *, lp, pad, want_h):
    m, d = h.shape
    tr = _pick(m, 520, 8)
    has_delta = delta is not None
    row_spec = pl.BlockSpec((tr, d), lambda i: (i, 0))
    in_specs = [row_spec] + ([row_spec] if has_delta else []) + [pl.BlockSpec((1, d), lambda i: (0, 0))]
    out_shape = ([jax.ShapeDtypeStruct((m, d), F32)] if want_h else []) + [jax.ShapeDtypeStruct((m, d), BF16)]
    out_specs = ([row_spec] if want_h else []) + [row_spec]
    args = [h] + ([delta] if has_delta else []) + [w.reshape(1, d)]
    outs = pl.pallas_call(
        functools.partial(_addnorm_kernel, lp=lp, pad=pad, has_delta=has_delta, want_h=want_h),
        grid=(m // tr,), in_specs=in_specs, out_specs=out_specs, out_shape=out_shape,
        compiler_params=_params(("parallel",), 12 * tr * d * 4),
        name="addnorm",
    )(*args)
    return outs if want_h else (None, outs[0])


def _final_norm_kernel(h_ref, d_ref, w_ref, o_ref):
    h = h_ref[0] + d_ref[0]
    o_ref[0] = h * lax.rsqrt(jnp.mean(h * h, axis=-1, keepdims=True) + EPS) * w_ref[...]


def _final_norm(h, delta, w, *, skip):
    b, lp, d = h.shape
    seq = lp - skip
    tr = skip
    assert seq % tr == 0
    in_spec = pl.BlockSpec((1, tr, d), lambda bi, i: (bi, i + 1, 0))
    return pl.pallas_call(
        _final_norm_kernel,
        grid=(b, seq // tr),
        in_specs=[in_spec, in_spec, pl.BlockSpec((1, d), lambda bi, i: (0, 0))],
        out_specs=pl.BlockSpec((1, tr, d), lambda bi, i: (bi, i, 0)),
        out_shape=jax.ShapeDtypeStruct((b, seq, d), F32),
        compiler_params=_params(("parallel", "parallel"), 12 * tr * d * 4),
        name="final_norm",
    )(h, delta, w.reshape(1, d))


def _mm_kernel(x_ref, w_ref, *rest):
    o_ref = rest[-1]
    y = jnp.dot(x_ref[...], w_ref[0].astype(BF16), preferred_element_type=F32)
    if len(rest) == 2:
        y = rest[0][...] + y
    o_ref[...] = y.astype(o_ref.dtype)


def _mm(x, w, layer, *, n_out=None, res=None, tn=512, out_dtype=F32, name="mm"):
    m, k = x.shape
    n_out = w.shape[2] if n_out is None else n_out
    tn = min(tn, n_out)
    assert n_out % tn == 0
    tm = _pick(m, 2080, 16)
    vmem = 2 * (tm * k * 2 + k * tn * 4 + 2 * tm * tn * 4) + k * tn * 2 + tm * tn * 4
    out_spec = pl.BlockSpec((tm, tn), lambda i, j: (i, j))
    return pl.pallas_call(
        _mm_kernel,
        grid=(m // tm, n_out // tn),
        in_specs=[pl.BlockSpec((tm, k), lambda i, j: (i, 0)),
                  pl.BlockSpec((1, k, tn), lambda i, j: (layer, 0, j))] + ([] if res is None else [out_spec]),
        out_specs=out_spec,
        out_shape=jax.ShapeDtypeStruct((m, n_out), out_dtype),
        compiler_params=_params(("parallel", "arbitrary"), vmem + (4 << 20)),
        name=name,
    )(*((x, w) if res is None else (x, w, res)))


def _mm_shift_kernel(x_ref, wa_ref, wb_ref, wl_ref, o_ref, *, shift):
    j = pl.program_id(1)
    tn = wa_ref.shape[2]
    nxt = jnp.where(j == pl.num_programs(1) - 1, wl_ref[0], wb_ref[0])
    win = jnp.concatenate([wa_ref[0], nxt], axis=1)
    w = pltpu.roll(win, tn + LANES - shift, axis=1)[:, 0:tn]
    o_ref[...] = jnp.dot(x_ref[...], w.astype(BF16), preferred_element_type=F32).astype(o_ref.dtype)


def _mm_shifted(x, w, layer, col0, shift, n_out, w_last, *, tn=512, out_dtype=F32, name="mm_shifted"):
    m, k = x.shape
    tn = min(tn, n_out)
    assert n_out % tn == 0 and col0 % tn == 0 and tn % LANES == 0 and 0 < shift < LANES
    assert col0 + n_out <= w.shape[2] and w_last.shape == (1, k, LANES)
    tm = _pick(m, 2080, 16)
    nj = n_out // tn
    per = tn // LANES
    nxt_max = (col0 + n_out) // LANES - 1
    vmem = 2 * (tm * k * 2 + k * (tn + 2 * LANES) * 4 + tm * tn * 4) + 3 * k * (tn + LANES) * 4 + tm * tn * 4
    return pl.pallas_call(
        functools.partial(_mm_shift_kernel, shift=shift),
        grid=(m // tm, nj),
        in_specs=[pl.BlockSpec((tm, k), lambda i, j: (i, 0)),
                  pl.BlockSpec((1, k, tn), lambda i, j: (layer, 0, col0 // tn + j)),
                  pl.BlockSpec((1, k, LANES), lambda i, j: (layer, 0, jnp.minimum(col0 // LANES + per * (j + 1), nxt_max))),
                  pl.BlockSpec((1, k, LANES), lambda i, j: (0, 0, 0))],
        out_specs=pl.BlockSpec((tm, tn), lambda i, j: (i, j)),
        out_shape=jax.ShapeDtypeStruct((m, n_out), out_dtype),
        compiler_params=_params(("parallel", "arbitrary"), vmem + (4 << 20)),
        name=name,
    )(x, w, w, w_last)


def _gdn_prep_kernel(q_ref, k_ref, v_ref, qh_ref, kh_ref, vh_ref, ba_ref, cw_ref, gp_ref,
                     u_ref, w_ref, qd_ref, kd_ref, qk_ref, gc_ref, ext_ref, *, heads, pad):
    c = CHUNK
    hd = heads * DN_DK
    n = pl.program_id(1)

    def conv_silu(main_ref, halo_ref, part):
        ext_ref[0:8, :] = halo_ref[0]
        ext_ref[8:8 + c, :] = main_ref[0]
        kw = cw_ref.shape[0]
        acc = jnp.zeros((c, hd), F32)
        for j in range(kw):
            acc = acc + cw_ref[j:j + 1, part * hd:(part + 1) * hd] * ext_ref[8 - (kw - 1) + j:8 - (kw - 1) + j + c, :]
        return _silu(acc)

    row = n * c + lax.broadcasted_iota(jnp.int32, (c, 1), 0)
    live = row >= pad
    beta = jnp.where(live, _sigmoid(ba_ref[0, :, 0:LANES]), 0.0)
    g = jnp.where(live, -jnp.exp(gp_ref[0:1, :]) * _softplus(ba_ref[0, :, LANES:2 * LANES] + gp_ref[1:2, :]), 0.0)

    ii = lax.broadcasted_iota(jnp.int32, (2 * c, c), 0)
    jj = lax.broadcasted_iota(jnp.int32, (2 * c, c), 1)
    tri_pad = jnp.where((ii >= jj) & (ii < c), 1.0, 0.0).astype(F32)
    gc_pad = jnp.dot(tri_pad, g, precision=HI, preferred_element_type=F32)
    gc = gc_pad[0:c]
    gct = gc_pad.T
    gc_ref[0] = gc
    eg = jnp.exp(gc)
    erev = jnp.exp(gc[c - 1:c, :] - gc)

    i2 = lax.broadcasted_iota(jnp.int32, (c, 2 * c), 0)
    j2 = lax.broadcasted_iota(jnp.int32, (c, 2 * c), 1)
    causal = i2 >= j2
    i1 = lax.broadcasted_iota(jnp.int32, (c, c), 0)
    j1 = lax.broadcasted_iota(jnp.int32, (c, c), 1)
    strict = i1 > j1

    qa = conv_silu(q_ref, qh_ref, 0)
    ka = conv_silu(k_ref, kh_ref, 1)
    va = conv_silu(v_ref, vh_ref, 2)

    def bdot(a, b):
        return jnp.dot(a.astype(BF16), b.astype(BF16), preferred_element_type=F32)

    a_list, rhs_list = [], []
    for h in range(heads):
        sl = slice(h * DN_DK, (h + 1) * DN_DK)
        qh = qa[:, sl]
        kh = ka[:, sl]
        qh = qh * lax.rsqrt(jnp.sum(qh * qh, axis=-1, keepdims=True) + EPS) * (DN_DK ** -0.5)
        kh = kh * lax.rsqrt(jnp.sum(kh * kh, axis=-1, keepdims=True) + EPS)
        bh = beta[:, h:h + 1]
        egh = eg[:, h:h + 1]
        kb = kh * bh
        decay = jnp.exp(jnp.where(causal, gc[:, h:h + 1] - gct[h:h + 1, :], NEG_BIG))
        k_pad = jnp.concatenate([kh, jnp.zeros_like(kh)], axis=0)
        lhs = jnp.concatenate([kb, qh], axis=0).astype(BF16)
        kq = lax.dot_general(lhs, k_pad.astype(BF16), (((1,), (1,)), ((), ())), preferred_element_type=F32)
        a_list.append(jnp.where(strict, kq[0:c, 0:c] * decay[:, 0:c], 0.0))
        qk_ref[0, :, h * 2 * c:(h + 1) * 2 * c] = (kq[c:2 * c, :] * decay).astype(qk_ref.dtype)
        rhs_list.append(jnp.concatenate([va[:, sl] * bh, kb * egh], axis=1))
        qd_ref[0, :, sl] = (qh * egh).astype(qd_ref.dtype)
        kd_ref[0, :, sl] = (kh * erev[:, h:h + 1]).astype(kd_ref.dtype)

    m_list = [-a for a in a_list]
    p_list = a_list
    span = 2
    while span < c:
        p_list = [bdot(p, p) for p in p_list]
        m_list = [mm + p + bdot(mm, p) for mm, p in zip(m_list, p_list)]
        span *= 2

    for h in range(heads):
        sl = slice(h * DN_DK, (h + 1) * DN_DK)
        sol = rhs_list[h] + bdot(m_list[h], rhs_list[h])
        u_ref[0, :, sl] = sol[:, 0:DN_DV]
        w_ref[0, :, sl] = sol[:, DN_DV:DN_DV + DN_DK].astype(w_ref.dtype)


def _gdn_prep(qkvz, ba, conv_w, gate_params, *, heads, pad):
    b, lp, _ = qkvz.shape
    hd = heads * DN_DK
    nc = lp // CHUNK
    main = lambda part: pl.BlockSpec((1, CHUNK, hd), lambda bi, n: (bi, n, part))
    halo = lambda part: pl.BlockSpec((1, 8, hd), lambda bi, n: (bi, jnp.maximum(n * (CHUNK // 8) - 1, 0), part))
    out_hd = pl.BlockSpec((1, CHUNK, hd), lambda bi, n: (bi, n, 0))
    shape = lambda width, dtype: jax.ShapeDtypeStruct((b, lp, width), dtype)
    return pl.pallas_call(
        functools.partial(_gdn_prep_kernel, heads=heads, pad=pad),
        grid=(b, nc),
        in_specs=[main(0), main(1), main(2), halo(0), halo(1), halo(2),
                  pl.BlockSpec((1, CHUNK, 2 * LANES), lambda bi, n: (bi, n, 0)),
                  pl.BlockSpec(conv_w.shape, lambda bi, n: (0, 0)),
                  pl.BlockSpec(gate_params.shape, lambda bi, n: (0, 0))],
        out_specs=[out_hd, out_hd, out_hd, out_hd,
                   pl.BlockSpec((1, CHUNK, heads * 2 * CHUNK), lambda bi, n: (bi, n, 0)),
                   pl.BlockSpec((1, CHUNK, LANES), lambda bi, n: (bi, n, 0))],
        out_shape=[shape(hd, F32), shape(hd, BF16), shape(hd, BF16), shape(hd, BF16),
                   shape(heads * 2 * CHUNK, BF16), shape(LANES, F32)],
        scratch_shapes=[pltpu.VMEM((8 + CHUNK, hd), F32)],
        compiler_params=_params(("parallel", "arbitrary"), 40 * CHUNK * hd * 4),
        name="gdn_prep",
    )(qkvz, qkvz, qkvz, qkvz, qkvz, qkvz, ba, conv_w, gate_params)


def _gdn_scan_kernel(u_ref, w_ref, qd_ref, kd_ref, qk_ref, gc_ref, z_ref, nw_ref, o_ref, s_ref, *, heads):
    c = CHUNK
    bsz = u_ref.shape[0]

    @pl.when(pl.program_id(0) == 0)
    def _():
        s_ref[...] = jnp.zeros_like(s_ref)

    for b in range(bsz):
        cd = jnp.exp(gc_ref[b, c - 1:c, :])
        for h in range(heads):
            sl = slice(h * DN_DV, (h + 1) * DN_DV)
            s = s_ref[b * heads + h]
            ws_qs = jnp.dot(jnp.concatenate([w_ref[b, :, sl], qd_ref[b, :, sl]], axis=0), s.astype(BF16),
                            preferred_element_type=F32)
            v_new = (u_ref[b, :, sl] - ws_qs[0:c]).astype(BF16)
            o = ws_qs[c:2 * c] + jnp.dot(qk_ref[b, :, h * 2 * c:h * 2 * c + c], v_new, preferred_element_type=F32)
            s_ref[b * heads + h] = s * cd[:, h:h + 1] + lax.dot_general(
                kd_ref[b, :, sl], v_new, (((0,), (0,)), ((), ())), preferred_element_type=F32)
            on = o * lax.rsqrt(jnp.mean(o * o, axis=-1, keepdims=True) + EPS) * nw_ref[...]
            o_ref[b, :, sl] = (on * _silu(z_ref[b, :, sl])).astype(o_ref.dtype)


def _gdn_scan(u, w, qd, kd, qk, gc, qkvz, norm_w, *, heads):
    b, lp, hd = u.shape
    nc = lp // CHUNK
    blk = lambda width, col: pl.BlockSpec((b, CHUNK, width), lambda n: (0, n, col))
    return pl.pallas_call(
        functools.partial(_gdn_scan_kernel, heads=heads),
        grid=(nc,),
        in_specs=[blk(hd, 0), blk(hd, 0), blk(hd, 0), blk(hd, 0), blk(heads * 2 * CHUNK, 0), blk(LANES, 0),
                  blk(hd, 3), pl.BlockSpec((1, DN_DV), lambda n: (0, 0))],
        out_specs=blk(hd, 0),
        out_shape=jax.ShapeDtypeStruct((b, lp, hd), BF16),
        scratch_shapes=[pltpu.VMEM((b * heads, DN_DK, DN_DV), F32)],
        compiler_params=_params(("arbitrary",), 24 * b * CHUNK * hd * 4),
        name="gdn_scan",
    )(u, w, qd, kd, qk, gc, qkvz, norm_w.reshape(1, DN_DV))


CONV_HALO = 32
CONV_CT = LANES
SUBLANES = 8


def _conformer_kernel(a_ref, g_ref, ah_ref, gh_ref, cw_ref, cb_ref, lg_ref, lb_ref, o_ref, ext_ref, sh_ref, y_ref):
    r, ch = o_ref.shape[1], o_ref.shape[2]
    kw = cw_ref.shape[0]
    ext_ref[0:CONV_HALO, :] = ah_ref[0].astype(F32) * _sigmoid(gh_ref[0].astype(F32))
    ext_ref[CONV_HALO:CONV_HALO + r, :] = a_ref[0].astype(F32) * _sigmoid(g_ref[0].astype(F32))
    base = CONV_HALO - (kw - 1)
    span = sh_ref.shape[1]

    def ch_body(ci, carry):
        c0 = pl.multiple_of(ci * CONV_CT, CONV_CT)
        for s in range(1, SUBLANES):
            sh_ref[s - 1] = ext_ref[s:s + span, pl.ds(c0, CONV_CT)]
        acc = jnp.broadcast_to(cb_ref[:, pl.ds(c0, CONV_CT)], (r, CONV_CT))
        for j in range(kw):
            q, s = divmod(base + j, SUBLANES)
            rows = slice(q * SUBLANES, q * SUBLANES + r)
            tap = sh_ref[s - 1, rows, :] if s else ext_ref[rows, pl.ds(c0, CONV_CT)]
            acc = acc + cw_ref[j:j + 1, pl.ds(c0, CONV_CT)] * tap
        y_ref[:, pl.ds(c0, CONV_CT)] = acc
        return carry

    lax.fori_loop(0, ch // CONV_CT, ch_body, 0)
    y = y_ref[...]
    mu = jnp.mean(y, axis=-1, keepdims=True)
    yc = y - mu
    var = jnp.mean(yc * yc, axis=-1, keepdims=True)
    yn = yc * lax.rsqrt(var + EPS) * lg_ref[...] + lb_ref[...]
    o_ref[0] = _silu(yn).astype(o_ref.dtype)


def _conformer(tail, conv_w, conv_b, ln_g, ln_b):
    b, lp, _ = tail.shape
    ch = conv_w.shape[1]
    assert conv_w.shape[0] - 1 <= CONV_HALO and ch % CONV_CT == 0
    r = _pick(lp, 320, CONV_HALO)
    main = lambda col: pl.BlockSpec((1, r, ch), lambda bi, i: (bi, i, col))
    halo = lambda col: pl.BlockSpec((1, CONV_HALO, ch), lambda bi, i: (bi, jnp.maximum(i * (r // CONV_HALO) - 1, 0), col))
    vec = pl.BlockSpec((1, ch), lambda bi, i: (0, 0))
    return pl.pallas_call(
        _conformer_kernel,
        grid=(b, lp // r),
        in_specs=[main(0), main(1), halo(0), halo(1), pl.BlockSpec(conv_w.shape, lambda bi, i: (0, 0)), vec, vec, vec],
        out_specs=pl.BlockSpec((1, r, ch), lambda bi, i: (bi, i, 0)),
        out_shape=jax.ShapeDtypeStruct((b, lp, ch), BF16),
        scratch_shapes=[pltpu.VMEM((CONV_HALO + r, ch), F32),
                        pltpu.VMEM((SUBLANES - 1, CONV_HALO + r - SUBLANES, CONV_CT), F32),
                        pltpu.VMEM((r, ch), F32)],
        compiler_params=_params(("parallel", "arbitrary"), 14 * r * ch * 4),
        name="conformer",
    )(tail, tail, tail, tail, conv_w, conv_b.reshape(1, ch), ln_g.reshape(1, ch), ln_b.reshape(1, ch))


def _branch_kernel(od_ref, oc_ref, wd_ref, wc_ref, gd_ref, gc_ref, o_ref):
    yd = jnp.dot(od_ref[...], wd_ref[0].astype(BF16), preferred_element_type=F32)
    yc = jnp.dot(oc_ref[...], wc_ref[0].astype(BF16), preferred_element_type=F32)
    o_ref[...] = (_sigmoid(gd_ref[...].astype(F32)) * yd + _sigmoid(gc_ref[...].astype(F32)) * yc).astype(o_ref.dtype)


def _branches(o_dn, o_cv, w_dn_out, w_conv_out, layer, tail, gate_col0):
    m, k = o_dn.shape
    d = w_dn_out.shape[2]
    assert o_cv.shape == o_dn.shape and w_conv_out.shape == w_dn_out.shape
    tn = min(512, d)
    tm = _pick(m, 1040, 16)
    g0 = gate_col0 // tn
    assert gate_col0 % tn == 0 and d % tn == 0
    x_spec = pl.BlockSpec((tm, k), lambda i, j: (i, 0))
    w_spec = pl.BlockSpec((1, k, tn), lambda i, j: (layer, 0, j))
    vmem = 2 * (2 * tm * k * 2 + 2 * k * tn * 4 + 2 * tm * tn * 4 + tm * tn * 2) + 2 * k * tn * 2 + 4 * tm * tn * 4
    return pl.pallas_call(
        _branch_kernel,
        grid=(m // tm, d // tn),
        in_specs=[x_spec, x_spec, w_spec, w_spec,
                  pl.BlockSpec((tm, tn), lambda i, j: (i, g0 + j)),
                  pl.BlockSpec((tm, tn), lambda i, j: (i, g0 + d // tn + j))],
        out_specs=pl.BlockSpec((tm, tn), lambda i, j: (i, j)),
        out_shape=jax.ShapeDtypeStruct((m, d), BF16),
        compiler_params=_params(("parallel", "arbitrary"), vmem + (4 << 20)),
        name="branches",
    )(o_dn, o_cv, w_dn_out, w_conv_out, tail, tail)


def _ffn_kernel(x_ref, w1_ref, w3_ref, w2_ref, o_ref):
    @pl.when(pl.program_id(1) == 0)
    def _():
        o_ref[...] = jnp.zeros_like(o_ref)

    x = x_ref[...]
    h1 = jnp.dot(x, w1_ref[0].astype(BF16), preferred_element_type=F32)
    h3 = jnp.dot(x, w3_ref[0].astype(BF16), preferred_element_type=F32)
    act = (_silu(h1) * h3).astype(BF16)
    o_ref[...] += jnp.dot(act, w2_ref[0].astype(BF16), preferred_element_type=F32)


def _ffn(x, w1, w3, w2, layer):
    m, d = x.shape
    ff = w1.shape[2]
    tm = _pick(m, 1040, 16)
    tf = _pick(ff, 256, LANES)
    vmem = 2 * (tm * d * 2 + 3 * d * tf * 4 + tm * d * 4) + 3 * d * tf * 2 + 4 * tm * tf * 4 + tm * d * 4
    return pl.pallas_call(
        _ffn_kernel,
        grid=(m // tm, ff // tf),
        in_specs=[pl.BlockSpec((tm, d), lambda i, f: (i, 0)),
                  pl.BlockSpec((1, d, tf), lambda i, f: (layer, 0, f)),
                  pl.BlockSpec((1, d, tf), lambda i, f: (layer, 0, f)),
                  pl.BlockSpec((1, tf, d), lambda i, f: (layer, f, 0))],
        out_specs=pl.BlockSpec((tm, d), lambda i, f: (i, 0)),
        out_shape=jax.ShapeDtypeStruct((m, d), F32),
        compiler_params=_params(("parallel", "arbitrary"), vmem + (4 << 20)),
        name="ffn",
    )(x, w1, w3, w2)


def _router_kernel(h_ref, nw_ref, rw_ref, u_ref, wt_ref, ix_ref, cnt_ref, *, n_exp):
    @pl.when(pl.program_id(0) == 0)
    def _():
        cnt_ref[...] = jnp.zeros_like(cnt_ref)

    h = h_ref[...]
    tr = h.shape[0]
    u = h * lax.rsqrt(jnp.mean(h * h, axis=-1, keepdims=True) + EPS) * nw_ref[...]
    u_ref[...] = u.astype(u_ref.dtype)
    logits = jnp.dot(u, rw_ref[...], precision=HI, preferred_element_type=F32)
    lane = lax.broadcasted_iota(jnp.int32, logits.shape, 1)
    logits = jnp.where(lane < n_exp, logits, NEG_BIG)
    m1 = jnp.max(logits, axis=-1, keepdims=True)
    i1 = jnp.min(jnp.where(logits == m1, lane, LANES), axis=-1, keepdims=True)
    rest = jnp.where(lane == i1, NEG_BIG, logits)
    m2 = jnp.max(rest, axis=-1, keepdims=True)
    i2 = jnp.min(jnp.where(rest == m2, lane, LANES), axis=-1, keepdims=True)
    e2 = jnp.exp(m2 - m1)
    den = 1.0 + e2
    wt_ref[...] = jnp.where(lane == 0, 1.0 / den, jnp.where(lane == 1, e2 / den, 0.0))

    chosen = jnp.where((lane == i1) | (lane == i2), 1.0, 0.0)
    ri = lax.broadcasted_iota(jnp.int32, (tr, tr), 0)
    ci = lax.broadcasted_iota(jnp.int32, (tr, tr), 1)
    before = jnp.where(ri > ci, 1.0, 0.0).astype(BF16)
    prefix = jnp.dot(before, chosen.astype(BF16), preferred_element_type=F32) + cnt_ref[0:1, :]
    ra = jnp.sum(jnp.where(lane == i1, prefix, 0.0), axis=-1, keepdims=True).astype(jnp.int32)
    rb = jnp.sum(jnp.where(lane == i2, prefix, 0.0), axis=-1, keepdims=True).astype(jnp.int32)
    ix_ref[...] = jnp.where(lane == 0, i1, jnp.where(lane == 1, i2, jnp.where(lane == 2, ra, jnp.where(lane == 3, rb, 0))))
    cnt_ref[...] = cnt_ref[...] + jnp.sum(chosen, axis=0, keepdims=True)


def _router(h, norm_w, router_w):
    m, d = h.shape
    n_exp = router_w.shape[1]
    assert 2 <= n_exp <= LANES
    rw = jnp.pad(router_w, ((0, 0), (0, LANES - n_exp)))
    tr = _pick(m, 520, 8)
    row_spec = pl.BlockSpec((tr, d), lambda i: (i, 0))
    lane_spec = pl.BlockSpec((tr, LANES), lambda i: (i, 0))
    return pl.pallas_call(
        functools.partial(_router_kernel, n_exp=n_exp),
        grid=(m // tr,),
        in_specs=[row_spec, pl.BlockSpec((1, d), lambda i: (0, 0)), pl.BlockSpec((d, LANES), lambda i: (0, 0))],
        out_specs=[row_spec, lane_spec, lane_spec, pl.BlockSpec((8, LANES), lambda i: (0, 0))],
        out_shape=[jax.ShapeDtypeStruct((m, d), BF16), jax.ShapeDtypeStruct((m, LANES), F32),
                   jax.ShapeDtypeStruct((m, LANES), jnp.int32), jax.ShapeDtypeStruct((8, LANES), F32)],
        compiler_params=_params(("arbitrary",), 12 * tr * d * 4),
        name="router",
    )(h, norm_w.reshape(1, d), rw)


MOE_TILE = 1152
MOE_SUBS = 3


def _moe_tile_rows(m):
    return MOE_TILE if m >= 4 * MOE_TILE else 96


def _dispatch_kernel(pa_ref, pb_ref, u_ref, z_ref, xs_ref, sem):
    del z_ref
    i = pl.program_id(0)
    tr = u_ref.shape[0]

    def copy(r, pos_ref):
        return pltpu.make_async_copy(u_ref.at[r], xs_ref.at[pos_ref[i * tr + r]], sem)

    def issue(r, carry):
        copy(r, pa_ref).start()
        copy(r, pb_ref).start()
        return carry

    lax.fori_loop(0, tr, issue, 0)
    for _ in range(2):
        pltpu.make_async_copy(u_ref, xs_ref.at[pl.ds(0, tr)], sem).wait()


def _dispatch(u3, pos_a, pos_b, n_rows):
    m, s, l = u3.shape
    tr = _pick(m, 520, 8)
    return pl.pallas_call(
        _dispatch_kernel,
        grid_spec=pltpu.PrefetchScalarGridSpec(
            num_scalar_prefetch=2, grid=(m // tr,),
            in_specs=[pl.BlockSpec((tr, s, l), lambda i, pa, pb: (i, 0, 0)), pl.BlockSpec(memory_space=pl.ANY)],
            out_specs=pl.BlockSpec(memory_space=pl.ANY),
            scratch_shapes=[pltpu.SemaphoreType.DMA]),
        out_shape=jax.ShapeDtypeStruct((n_rows, s, l), u3.dtype),
        input_output_aliases={3: 0},
        compiler_params=_params(("arbitrary",), 4 * tr * s * l * 2),
        name="moe_dispatch",
    )(pos_a, pos_b, u3, jnp.zeros((n_rows, s, l), u3.dtype))


def _moe_ffn_kernel(te_ref, rows_ref, src_ref, x_ref, w1_ref, w3_ref, w2_ref, o_ref, w1b_ref, w3b_ref, w2b_ref):
    del te_ref, src_ref
    s = pl.program_id(0)
    f = pl.program_id(1)
    rows = rows_ref[s]
    sub = x_ref.shape[0] // MOE_SUBS

    def block(i):
        sl = slice(i * sub, (i + 1) * sub)
        x = x_ref[sl, :]
        h1 = jnp.dot(x, w1b_ref[...], preferred_element_type=F32)
        h3 = jnp.dot(x, w3b_ref[...], preferred_element_type=F32)
        act = (_silu(h1) * h3).astype(BF16)
        o_ref[sl, :] += jnp.dot(act, w2b_ref[...], preferred_element_type=F32)

    @pl.when(f == 0)
    def _():
        o_ref[...] = jnp.zeros_like(o_ref)

    @pl.when(rows > 0)
    def _():
        w1b_ref[...] = w1_ref[0].astype(BF16)
        w3b_ref[...] = w3_ref[0].astype(BF16)
        w2b_ref[...] = w2_ref[0].astype(BF16)
        block(0)

    for i in range(1, MOE_SUBS):
        @pl.when(rows > i * sub)
        def _():
            block(i)


def _moe_ffn(xs, w1, w3, w2, tile_expert, tile_rows, tile_src, r):
    p, d = xs.shape
    ff = w1.shape[2]
    tf = _pick(ff, 256, LANES)
    nf = ff // tf
    f_of = lambda s, f, rows: jnp.where(rows[s] > 0, f, nf - 1)
    vmem = 2 * (r * d * 2 + 3 * d * tf * 4 + r * d * 4) + 3 * d * tf * 2 + 6 * (r // MOE_SUBS) * tf * 4
    return pl.pallas_call(
        _moe_ffn_kernel,
        grid_spec=pltpu.PrefetchScalarGridSpec(
            num_scalar_prefetch=3, grid=(p // r, nf),
            in_specs=[pl.BlockSpec((r, d), lambda s, f, te, rows, src: (src[s], 0)),
                      pl.BlockSpec((1, d, tf), lambda s, f, te, rows, src: (te[s], 0, f_of(s, f, rows))),
                      pl.BlockSpec((1, d, tf), lambda s, f, te, rows, src: (te[s], 0, f_of(s, f, rows))),
                      pl.BlockSpec((1, tf, d), lambda s, f, te, rows, src: (te[s], f_of(s, f, rows), 0))],
            out_specs=pl.BlockSpec((r, d), lambda s, f, te, rows, src: (s, 0)),
            scratch_shapes=[pltpu.VMEM((d, tf), BF16), pltpu.VMEM((d, tf), BF16), pltpu.VMEM((tf, d), BF16)]),
        out_shape=jax.ShapeDtypeStruct((p, d), F32),
        compiler_params=_params(("arbitrary", "arbitrary"), vmem + (4 << 20)),
        name="moe_ffn",
    )(tile_expert, tile_rows, tile_src, xs, w1, w3, w2)


def _combine_kernel(pa_ref, pb_ref, wt_ref, ys_ref, o_ref, bufa_ref, bufb_ref, sem):
    i = pl.program_id(0)
    tr = o_ref.shape[0]

    def copy(r, pos_ref, buf_ref):
        return pltpu.make_async_copy(ys_ref.at[pl.ds(pos_ref[i * tr + r], 1), :], buf_ref.at[pl.ds(r, 1), :], sem)

    def issue(r, carry):
        copy(r, pa_ref, bufa_ref).start()
        copy(r, pb_ref, bufb_ref).start()
        return carry

    lax.fori_loop(0, tr, issue, 0)
    pltpu.make_async_copy(ys_ref.at[pl.ds(0, tr), :], bufa_ref, sem).wait()
    pltpu.make_async_copy(ys_ref.at[pl.ds(0, tr), :], bufb_ref, sem).wait()
    o_ref[...] = wt_ref[:, 0:1] * bufa_ref[...] + wt_ref[:, 1:2] * bufb_ref[...]


def _combine(ys, wt, pos_a, pos_b):
    m = wt.shape[0]
    d = ys.shape[1]
    tr = _pick(m, 520, 8)
    return pl.pallas_call(
        _combine_kernel,
        grid_spec=pltpu.PrefetchScalarGridSpec(
            num_scalar_prefetch=2, grid=(m // tr,),
            in_specs=[pl.BlockSpec((tr, LANES), lambda i, pa, pb: (i, 0)), pl.BlockSpec(memory_space=pl.ANY)],
            out_specs=pl.BlockSpec((tr, d), lambda i, pa, pb: (i, 0)),
            scratch_shapes=[pltpu.VMEM((tr, d), F32), pltpu.VMEM((tr, d), F32), pltpu.SemaphoreType.DMA]),
        out_shape=jax.ShapeDtypeStruct((m, d), F32),
        compiler_params=_params(("arbitrary",), 8 * tr * d * 4),
        name="moe_combine",
    )(pos_a, pos_b, wt, ys)


def _moe(h, norm_w, router_w, w1, w3, w2, e0):
    m, d = h.shape
    n_exp = router_w.shape[1]
    r = _moe_tile_rows(m)
    n_tiles = (2 * m + n_exp * (r - 1) + r - 1) // r
    u, wt, ix, cnt = _router(h, norm_w, router_w)

    cnt = cnt[0, :n_exp].astype(jnp.int32)
    nt = (cnt + r - 1) // r
    tend = jnp.cumsum(nt)
    tstart = tend - nt
    pos_a = tstart[ix[:, 0]] * r + ix[:, 2]
    pos_b = tstart[ix[:, 1]] * r + ix[:, 3]
    n_used = tend[n_exp - 1]
    s = jnp.arange(n_tiles, dtype=jnp.int32)
    tile_src = jnp.clip(s, 0, jnp.maximum(n_used - 1, 0))
    tile_expert = jnp.minimum(jnp.searchsorted(tend, tile_src, side="right"), n_exp - 1).astype(jnp.int32)
    tile_rows = jnp.where(s < n_used, jnp.clip(cnt[tile_expert] - (s - tstart[tile_expert]) * r, 0, r), 0).astype(jnp.int32)

    assert d % LANES == 0
    xs = _dispatch(u.reshape(m, d // LANES, LANES), pos_a, pos_b, n_tiles * r)
    ys = _moe_ffn(xs.reshape(n_tiles * r, d), w1, w3, w2, tile_expert + e0, tile_rows, tile_src, r)
    return _combine(ys, wt, pos_a, pos_b)


def kernel(x, meta_tokens, attn_norm, w_in, short_conv_w, a_log, dt_bias, dn_norm, w_dn_out, dw_conv_w, dw_conv_b,
           conv_ln_g, conv_ln_b, w_conv_out, w_merge_out, ffn_norm, dense_w1, dense_w3, dense_w2, router_w,
           moe_w1, moe_w3, moe_w2, final_norm):
    bsz, seq, d = x.shape
    n_meta = meta_tokens.shape[0]
    depth, heads = a_log.shape
    pad = (-n_meta) % CHUNK
    lp = pad + n_meta + seq
    m = bsz * lp
    hd = heads * DN_DK
    ch = dw_conv_w.shape[2]
    assert lp % CHUNK == 0 and DN_DK == DN_DV == LANES and heads <= LANES
    off_b = 3 * hd + heads * DN_DV
    off_glu = off_b + 2 * heads
    assert w_in.shape[2] == off_glu + 2 * ch + 2 * d

    meta = jnp.broadcast_to(meta_tokens[None].astype(x.dtype), (bsz, n_meta, d))
    h = jnp.concatenate([jnp.zeros((bsz, pad, d), x.dtype), meta, x], axis=1).reshape(m, d)

    delta = None
    for i in range(depth):
        h_new, u = _addnorm(h, delta, attn_norm[i], lp=lp, pad=pad, want_h=delta is not None)
        h = h if delta is None else h_new
        qkvz = _mm(u, w_in, i, n_out=off_b, name="in_proj_qkvz")
        n_tail = 2 * ch + 2 * d
        w_last = jnp.pad(w_in[i:i + 1, :, off_b + n_tail:], ((0, 0), (0, 0), (0, LANES - 2 * heads)))
        tail = _mm_shifted(u, w_in, i, off_b, 2 * heads, n_tail, w_last, out_dtype=BF16,
                           name="in_proj_tail")
        w_ba = jnp.zeros((1, d, 2 * LANES), F32)
        w_ba = w_ba.at[0, :, 0:heads].set(w_in[i, :, off_b:off_b + heads])
        w_ba = w_ba.at[0, :, LANES:LANES + heads].set(w_in[i, :, off_b + heads:off_glu])
        ba = _mm(u, w_ba, 0, name="in_proj_ba")
        gate_params = jnp.zeros((8, LANES), F32)
        gate_params = gate_params.at[0, 0:heads].set(a_log[i]).at[1, 0:heads].set(dt_bias[i])

        qkvz3 = qkvz.reshape(bsz, lp, off_b)
        tail3 = tail.reshape(bsz, lp, 2 * ch + 2 * d)
        prep = _gdn_prep(qkvz3, ba.reshape(bsz, lp, 2 * LANES), short_conv_w[i], gate_params, heads=heads, pad=pad)
        o_dn = _gdn_scan(*prep, qkvz3, dn_norm[i], heads=heads)
        o_cv = _conformer(tail3, dw_conv_w[i], dw_conv_b[i], conv_ln_g[i], conv_ln_b[i])
        merged = _branches(o_dn.reshape(m, hd), o_cv.reshape(m, ch), w_dn_out, w_conv_out, i, tail, 2 * ch)
        h = _mm(merged, w_merge_out, i, res=h, name="merge_out")

        j = i // 2
        if i % 2 == 0:
            _, u = _addnorm(h, None, ffn_norm[i], lp=lp, pad=0, want_h=False)
            delta = _ffn(u, dense_w1, dense_w3, dense_w2, j)
        else:
            n_exp = moe_w1.shape[1]
            flat = lambda t: t.reshape((-1,) + t.shape[2:])
            delta = _moe(h, ffn_norm[i], router_w[j], flat(moe_w1), flat(moe_w3), flat(moe_w2), j * n_exp)

    return _final_norm(h.reshape(bsz, lp, d), delta.reshape(bsz, lp, d), final_norm, skip=pad + n_meta)
```

```python
import functools

import jax
import jax.numpy as jnp
from jax import lax
from jax.experimental import pallas as pl
from jax.experimental.pallas import tpu as pltpu

DN_DK = 128
DN_DV = 128
CHUNK = 64
EPS = 1e-6
LANES = 128
SUBLANES = 8
NEG_BIG = -1e30
V7X_VMEM_BYTES = 64 * 1024 * 1024
VMEM_CAP = 56 * 1024 * 1024

F32 = jnp.float32
BF16 = jnp.bfloat16
HI = lax.Precision.HIGHEST


def _pick(n, target, mult):
    best = None
    for d in range(mult, min(n, target) + 1, mult):
        if n % d == 0:
            best = d
    assert best is not None, (n, target, mult)
    return best


def _params(sem, vmem_bytes):
    return pltpu.CompilerParams(dimension_semantics=sem,
                                vmem_limit_bytes=int(min(VMEM_CAP, max(vmem_bytes, 16 * 1024 * 1024))))


def _sigmoid(x):
    return 0.5 * jnp.tanh(0.5 * x) + 0.5


def _silu(x):
    return x * _sigmoid(x)


def _softplus(x):
    return jnp.maximum(x, 0.0) + jnp.log(1.0 + jnp.exp(-jnp.abs(x)))


def _addnorm_kernel(*refs, lp, pad, has_delta, want_h):
    it = iter(refs)
    h_ref = next(it)
    d_ref = next(it) if has_delta else None
    w_ref = next(it)
    hn_ref = next(it) if want_h else None
    u_ref = next(it)
    tr = h_ref.shape[0]
    h = h_ref[...]
    if has_delta:
        h = h + d_ref[...]
    if want_h:
        hn_ref[...] = h
    y = h * lax.rsqrt(jnp.mean(h * h, axis=-1, keepdims=True) + EPS) * w_ref[...]
    row = pl.program_id(0) * tr + lax.broadcasted_iota(jnp.int32, (tr, 1), 0)
    y = jnp.where(row % lp >= pad, y, 0.0)
    u_ref[...] = y.astype(u_ref.dtype)


def _addnorm(h, delta, w, *, lp, pad, want_h):
    m, d = h.shape
    tr = _pick(m, 520, 8)
    has_delta = delta is not None
    row_spec = pl.BlockSpec((tr, d), lambda i: (i, 0))
    in_specs = [row_spec] + ([row_spec] if has_delta else []) + [pl.BlockSpec((1, d), lambda i: (0, 0))]
    out_shape = ([jax.ShapeDtypeStruct((m, d), F32)] if want_h else []) + [jax.ShapeDtypeStruct((m, d), BF16)]
    out_specs = ([row_spec] if want_h else []) + [row_spec]
    args = [h] + ([delta] if has_delta else []) + [w.reshape(1, d)]
    outs = pl.pallas_call(
        functools.partial(_addnorm_kernel, lp=lp, pad=pad, has_delta=has_delta, want_h=want_h),
        grid=(m // tr,), in_specs=in_specs, out_specs=out_specs, out_shape=out_shape,
        compiler_params=_params(("parallel",), 12 * tr * d * 4),
        name="addnorm",
    )(*args)
    return outs if want_h else (None, outs[0])


def _final_norm_kernel(h_ref, d_ref, w_ref, o_ref):
    h = h_ref[0] + d_ref[0]
    o_ref[0] = h * lax.rsqrt(jnp.mean(h * h, axis=-1, keepdims=True) + EPS) * w_ref[...]


def _final_norm(h, delta, w, *, skip):
    b, lp, d = h.shape
    seq = lp - skip
    tr = _pick(seq, 512, SUBLANES)
    assert skip % SUBLANES == 0
    in_spec = pl.BlockSpec((pl.Element(1), pl.Element(tr), pl.Element(d)),
                           lambda bi, i: (bi, pl.multiple_of(skip + i * tr, SUBLANES), 0))
    return pl.pallas_call(
        _final_norm_kernel,
        grid=(b, seq // tr),
        in_specs=[in_spec, in_spec, pl.BlockSpec((1, d), lambda bi, i: (0, 0))],
        out_specs=pl.BlockSpec((1, tr, d), lambda bi, i: (bi, i, 0)),
        out_shape=jax.ShapeDtypeStruct((b, seq, d), F32),
        compiler_params=_params(("parallel", "parallel"), 12 * tr * d * 4),
        name="final_norm",
    )(h, delta, w.reshape(1, d))


def _mm_kernel(x_ref, w_ref, *rest):
    o_ref = rest[-1]
    y = jnp.dot(x_ref[...], w_ref[0].astype(BF16), preferred_element_type=F32)
    if len(rest) == 2:
        y = rest[0][...] + y
    o_ref[...] = y.astype(o_ref.dtype)


def _mm(x, w, layer, *, n_out=None, res=None, tn=512, out_dtype=F32, name="mm"):
    m, k = x.shape
    n_out = w.shape[2] if n_out is None else n_out
    tn = min(tn, n_out)
    assert n_out % tn == 0
    tm = _pick(m, 2080, 16)
    vmem = 2 * (tm * k * 2 + k * tn * 4 + 2 * tm * tn * 4) + k * tn * 2 + tm * tn * 4
    out_spec = pl.BlockSpec((tm, tn), lambda i, j: (i, j))
    return pl.pallas_call(
        _mm_kernel,
        grid=(m // tm, n_out // tn),
        in_specs=[pl.BlockSpec((tm, k), lambda i, j: (i, 0)),
                  pl.BlockSpec((1, k, tn), lambda i, j: (layer, 0, j))] + ([] if res is None else [out_spec]),
        out_specs=out_spec,
        out_shape=jax.ShapeDtypeStruct((m, n_out), out_dtype),
        compiler_params=_params(("parallel", "arbitrary"), vmem + (4 << 20)),
        name=name,
    )(*((x, w) if res is None else (x, w, res)))


def _mm_nt_kernel(x_ref, wt_ref, o_ref):
    y = lax.dot_general(x_ref[...], wt_ref[0].astype(BF16), (((1,), (1,)), ((), ())), preferred_element_type=F32)
    o_ref[...] = y.astype(o_ref.dtype)


def _mm_nt(x, wt, layer, row0, n_out, *, tn=512, out_dtype=F32, name="mm_nt"):
    m, k = x.shape
    tn = min(tn, n_out)
    assert n_out % tn == 0 and row0 % SUBLANES == 0 and row0 + n_out <= wt.shape[1]
    tm = _pick(m, 2080, 16)
    vmem = 2 * (tm * k * 2 + k * tn * 4 + tm * tn * 4) + 2 * k * tn * 2 + tm * tn * 4
    return pl.pallas_call(
        _mm_nt_kernel,
        grid=(m // tm, n_out // tn),
        in_specs=[pl.BlockSpec((tm, k), lambda i, j: (i, 0)),
                  pl.BlockSpec((pl.Element(1), pl.Element(tn), pl.Element(k)), lambda i, j: (layer, pl.multiple_of(row0 + j * tn, SUBLANES), 0))],
        out_specs=pl.BlockSpec((tm, tn), lambda i, j: (i, j)),
        out_shape=jax.ShapeDtypeStruct((m, n_out), out_dtype),
        compiler_params=_params(("parallel", "arbitrary"), vmem + (4 << 20)),
        name=name,
    )(x, wt)


def _gdn_prep_kernel(q_ref, k_ref, v_ref, qh_ref, kh_ref, vh_ref, ba_ref, cw_ref, gp_ref,
                     u_ref, w_ref, qd_ref, kd_ref, qk_ref, gc_ref, ext_ref, *, heads, pad):
    c = CHUNK
    hd = heads * DN_DK
    n = pl.program_id(1)

    def conv_silu(main_ref, halo_ref, part):
        ext_ref[0:8, :] = halo_ref[0]
        ext_ref[8:8 + c, :] = main_ref[0]
        kw = cw_ref.shape[0]
        acc = jnp.zeros((c, hd), F32)
        for j in range(kw):
            acc = acc + cw_ref[j:j + 1, part * hd:(part + 1) * hd] * ext_ref[8 - (kw - 1) + j:8 - (kw - 1) + j + c, :]
        return _silu(acc)

    row = n * c + lax.broadcasted_iota(jnp.int32, (c, 1), 0)
    live = row >= pad
    ba = ba_ref[0]
    alpha = pltpu.roll(ba, LANES - heads, axis=1)
    beta = jnp.where(live, _sigmoid(ba), 0.0)
    g = jnp.where(live, -jnp.exp(gp_ref[0:1, :]) * _softplus(alpha + gp_ref[1:2, :]), 0.0)

    ii = lax.broadcasted_iota(jnp.int32, (2 * c, c), 0)
    jj = lax.broadcasted_iota(jnp.int32, (2 * c, c), 1)
    tri_pad = jnp.where((ii >= jj) & (ii < c), 1.0, 0.0).astype(F32)
    gc_pad = jnp.dot(tri_pad, g, precision=HI, preferred_element_type=F32)
    gc = gc_pad[0:c]
    gct = gc_pad.T
    gc_ref[0] = gc
    eg = jnp.exp(gc)
    erev = jnp.exp(gc[c - 1:c, :] - gc)

    i2 = lax.broadcasted_iota(jnp.int32, (c, 2 * c), 0)
    j2 = lax.broadcasted_iota(jnp.int32, (c, 2 * c), 1)
    causal = i2 >= j2
    i1 = lax.broadcasted_iota(jnp.int32, (c, c), 0)
    j1 = lax.broadcasted_iota(jnp.int32, (c, c), 1)
    strict = i1 > j1

    qa = conv_silu(q_ref, qh_ref, 0)
    ka = conv_silu(k_ref, kh_ref, 1)
    va = conv_silu(v_ref, vh_ref, 2)

    def bdot(a, b):
        return jnp.dot(a.astype(BF16), b.astype(BF16), preferred_element_type=F32)

    a_list, rhs_list = [], []
    for h in range(heads):
        sl = slice(h * DN_DK, (h + 1) * DN_DK)
        qh = qa[:, sl]
        kh = ka[:, sl]
        qh = qh * lax.rsqrt(jnp.sum(qh * qh, axis=-1, keepdims=True) + EPS) * (DN_DK ** -0.5)
        kh = kh * lax.rsqrt(jnp.sum(kh * kh, axis=-1, keepdims=True) + EPS)
        bh = beta[:, h:h + 1]
        egh = eg[:, h:h + 1]
        kb = kh * bh
        decay = jnp.exp(jnp.where(causal, gc[:, h:h + 1] - gct[h:h + 1, :], NEG_BIG))
        k_pad = jnp.concatenate([kh, jnp.zeros_like(kh)], axis=0)
        lhs = jnp.concatenate([kb, qh], axis=0).astype(BF16)
        kq = lax.dot_general(lhs, k_pad.astype(BF16), (((1,), (1,)), ((), ())), preferred_element_type=F32)
        a_list.append(jnp.where(strict, kq[0:c, 0:c] * decay[:, 0:c], 0.0))
        qk_ref[0, :, h * 2 * c:(h + 1) * 2 * c] = (kq[c:2 * c, :] * decay).astype(qk_ref.dtype)
        rhs_list.append(jnp.concatenate([va[:, sl] * bh, kb * egh], axis=1))
        qd_ref[0, :, sl] = (qh * egh).astype(qd_ref.dtype)
        kd_ref[0, :, sl] = (kh * erev[:, h:h + 1]).astype(kd_ref.dtype)

    m_list = [-a for a in a_list]
    p_list = a_list
    span = 2
    while span < c:
        p_list = [bdot(p, p) for p in p_list]
        m_list = [mm + p + bdot(mm, p) for mm, p in zip(m_list, p_list)]
        span *= 2

    for h in range(heads):
        sl = slice(h * DN_DK, (h + 1) * DN_DK)
        sol = rhs_list[h] + bdot(m_list[h], rhs_list[h])
        u_ref[0, :, sl] = sol[:, 0:DN_DV]
        w_ref[0, :, sl] = sol[:, DN_DV:DN_DV + DN_DK].astype(w_ref.dtype)


def _gdn_prep(qkvz, ba, conv_w, gate_params, *, heads, pad):
    b, lp, _ = qkvz.shape
    hd = heads * DN_DK
    nc = lp // CHUNK
    main = lambda part: pl.BlockSpec((1, CHUNK, hd), lambda bi, n: (bi, n, part))
    halo = lambda part: pl.BlockSpec((1, 8, hd), lambda bi, n: (bi, jnp.maximum(n * (CHUNK // 8) - 1, 0), part))
    out_hd = pl.BlockSpec((1, CHUNK, hd), lambda bi, n: (bi, n, 0))
    shape = lambda width, dtype: jax.ShapeDtypeStruct((b, lp, width), dtype)
    return pl.pallas_call(
        functools.partial(_gdn_prep_kernel, heads=heads, pad=pad),
        grid=(b, nc),
        in_specs=[main(0), main(1), main(2), halo(0), halo(1), halo(2),
                  pl.BlockSpec((1, CHUNK, LANES), lambda bi, n: (bi, n, 0)),
                  pl.BlockSpec(conv_w.shape, lambda bi, n: (0, 0)),
                  pl.BlockSpec(gate_params.shape, lambda bi, n: (0, 0))],
        out_specs=[out_hd, out_hd, out_hd, out_hd,
                   pl.BlockSpec((1, CHUNK, heads * 2 * CHUNK), lambda bi, n: (bi, n, 0)),
                   pl.BlockSpec((1, CHUNK, LANES), lambda bi, n: (bi, n, 0))],
        out_shape=[shape(hd, F32), shape(hd, BF16), shape(hd, BF16), shape(hd, BF16),
                   shape(heads * 2 * CHUNK, BF16), shape(LANES, F32)],
        scratch_shapes=[pltpu.VMEM((8 + CHUNK, hd), F32)],
        compiler_params=_params(("parallel", "arbitrary"), 40 * CHUNK * hd * 4),
        name="gdn_prep",
    )(qkvz, qkvz, qkvz, qkvz, qkvz, qkvz, ba, conv_w, gate_params)


def _gdn_scan_kernel(u_ref, w_ref, qd_ref, kd_ref, qk_ref, gc_ref, z_ref, nw_ref, o_ref, s_ref, *, heads):
    c = CHUNK
    bsz = u_ref.shape[0]

    @pl.when(pl.program_id(0) == 0)
    def _():
        s_ref[...] = jnp.zeros_like(s_ref)

    for b in range(bsz):
        cd = jnp.exp(gc_ref[b, c - 1:c, :])
        for h in range(heads):
            sl = slice(h * DN_DV, (h + 1) * DN_DV)
            s = s_ref[b * heads + h]
            ws_qs = jnp.dot(jnp.concatenate([w_ref[b, :, sl], qd_ref[b, :, sl]], axis=0), s.astype(BF16),
                            preferred_element_type=F32)
            v_new = (u_ref[b, :, sl] - ws_qs[0:c]).astype(BF16)
            o = ws_qs[c:2 * c] + jnp.dot(qk_ref[b, :, h * 2 * c:h * 2 * c + c], v_new, preferred_element_type=F32)
            s_ref[b * heads + h] = s * cd[:, h:h + 1] + lax.dot_general(
                kd_ref[b, :, sl], v_new, (((0,), (0,)), ((), ())), preferred_element_type=F32)
            on = o * lax.rsqrt(jnp.mean(o * o, axis=-1, keepdims=True) + EPS) * nw_ref[...]
            o_ref[b, :, sl] = (on * _silu(z_ref[b, :, sl])).astype(o_ref.dtype)


def _gdn_scan(u, w, qd, kd, qk, gc, qkvz, norm_w, *, heads):
    b, lp, hd = u.shape
    nc = lp // CHUNK
    blk = lambda width, col: pl.BlockSpec((b, CHUNK, width), lambda n: (0, n, col))
    return pl.pallas_call(
        functools.partial(_gdn_scan_kernel, heads=heads),
        grid=(nc,),
        in_specs=[blk(hd, 0), blk(hd, 0), blk(hd, 0), blk(hd, 0), blk(heads * 2 * CHUNK, 0), blk(LANES, 0),
                  blk(hd, 3), pl.BlockSpec((1, DN_DV), lambda n: (0, 0))],
        out_specs=blk(hd, 0),
        out_shape=jax.ShapeDtypeStruct((b, lp, hd), BF16),
        scratch_shapes=[pltpu.VMEM((b * heads, DN_DK, DN_DV), F32)],
        compiler_params=_params(("arbitrary",), 24 * b * CHUNK * hd * 4),
        name="gdn_scan",
    )(u, w, qd, kd, qk, gc, qkvz, norm_w.reshape(1, DN_DV))


CONV_HALO = 32
CONV_CT = LANES


def _conformer_kernel(a_ref, g_ref, ah_ref, gh_ref, cw_ref, cb_ref, lg_ref, lb_ref, o_ref, ext_ref, sh_ref, y_ref):
    r, ch = o_ref.shape[1], o_ref.shape[2]
    kw = cw_ref.shape[0]
    ext_ref[0:CONV_HALO, :] = ah_ref[0].astype(F32) * _sigmoid(gh_ref[0].astype(F32))
    ext_ref[CONV_HALO:CONV_HALO + r, :] = a_ref[0].astype(F32) * _sigmoid(g_ref[0].astype(F32))
    base = CONV_HALO - (kw - 1)
    span = sh_ref.shape[1]

    def ch_body(ci, carry):
        c0 = pl.multiple_of(ci * CONV_CT, CONV_CT)
        for s in range(1, SUBLANES):
            sh_ref[s - 1] = ext_ref[s:s + span, pl.ds(c0, CONV_CT)]
        acc = jnp.broadcast_to(cb_ref[:, pl.ds(c0, CONV_CT)], (r, CONV_CT))
        for j in range(kw):
            q, s = divmod(base + j, SUBLANES)
            rows = slice(q * SUBLANES, q * SUBLANES + r)
            tap = sh_ref[s - 1, rows, :] if s else ext_ref[rows, pl.ds(c0, CONV_CT)]
            acc = acc + cw_ref[j:j + 1, pl.ds(c0, CONV_CT)] * tap
        y_ref[:, pl.ds(c0, CONV_CT)] = acc
        return carry

    lax.fori_loop(0, ch // CONV_CT, ch_body, 0)
    y = y_ref[...]
    mu = jnp.mean(y, axis=-1, keepdims=True)
    yc = y - mu
    var = jnp.mean(yc * yc, axis=-1, keepdims=True)
    yn = yc * lax.rsqrt(var + EPS) * lg_ref[...] + lb_ref[...]
    o_ref[0] = _silu(yn).astype(o_ref.dtype)


def _conformer(tail, conv_w, conv_b, ln_g, ln_b):
    b, lp, _ = tail.shape
    ch = conv_w.shape[1]
    assert conv_w.shape[0] - 1 <= CONV_HALO and ch % CONV_CT == 0
    r = _pick(lp, 320, CONV_HALO)
    main = lambda col: pl.BlockSpec((1, r, ch), lambda bi, i: (bi, i, col))
    halo = lambda col: pl.BlockSpec((1, CONV_HALO, ch), lambda bi, i: (bi, jnp.maximum(i * (r // CONV_HALO) - 1, 0), col))
    vec = pl.BlockSpec((1, ch), lambda bi, i: (0, 0))
    return pl.pallas_call(
        _conformer_kernel,
        grid=(b, lp // r),
        in_specs=[main(0), main(1), halo(0), halo(1), pl.BlockSpec(conv_w.shape, lambda bi, i: (0, 0)), vec, vec, vec],
        out_specs=pl.BlockSpec((1, r, ch), lambda bi, i: (bi, i, 0)),
        out_shape=jax.ShapeDtypeStruct((b, lp, ch), BF16),
        scratch_shapes=[pltpu.VMEM((CONV_HALO + r, ch), F32),
                        pltpu.VMEM((SUBLANES - 1, CONV_HALO + r - SUBLANES, CONV_CT), F32),
                        pltpu.VMEM((r, ch), F32)],
        compiler_params=_params(("parallel", "arbitrary"), 14 * r * ch * 4),
        name="conformer",
    )(tail, tail, tail, tail, conv_w, conv_b.reshape(1, ch), ln_g.reshape(1, ch), ln_b.reshape(1, ch))


def _branch_kernel(od_ref, oc_ref, wd_ref, wc_ref, gd_ref, gc_ref, o_ref):
    yd = jnp.dot(od_ref[...], wd_ref[0].astype(BF16), preferred_element_type=F32)
    yc = jnp.dot(oc_ref[...], wc_ref[0].astype(BF16), preferred_element_type=F32)
    o_ref[...] = (_sigmoid(gd_ref[...].astype(F32)) * yd + _sigmoid(gc_ref[...].astype(F32)) * yc).astype(o_ref.dtype)


def _branches(o_dn, o_cv, w_dn_out, w_conv_out, layer, tail, gate_col0):
    m, k = o_dn.shape
    d = w_dn_out.shape[2]
    assert o_cv.shape == o_dn.shape and w_conv_out.shape == w_dn_out.shape
    tn = min(512, d)
    tm = _pick(m, 1040, 16)
    g0 = gate_col0 // tn
    assert gate_col0 % tn == 0 and d % tn == 0
    x_spec = pl.BlockSpec((tm, k), lambda i, j: (i, 0))
    w_spec = pl.BlockSpec((1, k, tn), lambda i, j: (layer, 0, j))
    vmem = 2 * (2 * tm * k * 2 + 2 * k * tn * 4 + 2 * tm * tn * 4 + tm * tn * 2) + 2 * k * tn * 2 + 4 * tm * tn * 4
    return pl.pallas_call(
        _branch_kernel,
        grid=(m // tm, d // tn),
        in_specs=[x_spec, x_spec, w_spec, w_spec,
                  pl.BlockSpec((tm, tn), lambda i, j: (i, g0 + j)),
                  pl.BlockSpec((tm, tn), lambda i, j: (i, g0 + d // tn + j))],
        out_specs=pl.BlockSpec((tm, tn), lambda i, j: (i, j)),
        out_shape=jax.ShapeDtypeStruct((m, d), BF16),
        compiler_params=_params(("parallel", "arbitrary"), vmem + (4 << 20)),
        name="branches",
    )(o_dn, o_cv, w_dn_out, w_conv_out, tail, tail)


def _ffn_kernel(x_ref, w1_ref, w3_ref, w2_ref, o_ref):
    @pl.when(pl.program_id(1) == 0)
    def _():
        o_ref[...] = jnp.zeros_like(o_ref)

    x = x_ref[...]
    h1 = jnp.dot(x, w1_ref[0].astype(BF16), preferred_element_type=F32)
    h3 = jnp.dot(x, w3_ref[0].astype(BF16), preferred_element_type=F32)
    act = (_silu(h1) * h3).astype(BF16)
    o_ref[...] += jnp.dot(act, w2_ref[0].astype(BF16), preferred_element_type=F32)


def _ffn(x, w1, w3, w2, layer):
    m, d = x.shape
    ff = w1.shape[2]
    tm = _pick(m, 1040, 16)
    tf = _pick(ff, 256, LANES)
    vmem = 2 * (tm * d * 2 + 3 * d * tf * 4 + tm * d * 4) + 3 * d * tf * 2 + 4 * tm * tf * 4 + tm * d * 4
    return pl.pallas_call(
        _ffn_kernel,
        grid=(m // tm, ff // tf),
        in_specs=[pl.BlockSpec((tm, d), lambda i, f: (i, 0)),
                  pl.BlockSpec((1, d, tf), lambda i, f: (layer, 0, f)),
                  pl.BlockSpec((1, d, tf), lambda i, f: (layer, 0, f)),
                  pl.BlockSpec((1, tf, d), lambda i, f: (layer, f, 0))],
        out_specs=pl.BlockSpec((tm, d), lambda i, f: (i, 0)),
        out_shape=jax.ShapeDtypeStruct((m, d), F32),
        compiler_params=_params(("parallel", "arbitrary"), vmem + (4 << 20)),
        name="ffn",
    )(x, w1, w3, w2)


def _router_kernel(h_ref, nw_ref, rw_ref, u_ref, wt_ref, ix_ref, cnt_ref, *, n_exp):
    @pl.when(pl.program_id(0) == 0)
    def _():
        cnt_ref[...] = jnp.zeros_like(cnt_ref)

    h = h_ref[...]
    tr = h.shape[0]
    u = h * lax.rsqrt(jnp.mean(h * h, axis=-1, keepdims=True) + EPS) * nw_ref[...]
    u_ref[...] = u.astype(u_ref.dtype)
    logits = jnp.dot(u, rw_ref[...], precision=HI, preferred_element_type=F32)
    lane = lax.broadcasted_iota(jnp.int32, logits.shape, 1)
    logits = jnp.where(lane < n_exp, logits, NEG_BIG)
    m1 = jnp.max(logits, axis=-1, keepdims=True)
    i1 = jnp.min(jnp.where(logits == m1, lane, LANES), axis=-1, keepdims=True)
    rest = jnp.where(lane == i1, NEG_BIG, logits)
    m2 = jnp.max(rest, axis=-1, keepdims=True)
    i2 = jnp.min(jnp.where(rest == m2, lane, LANES), axis=-1, keepdims=True)
    e2 = jnp.exp(m2 - m1)
    den = 1.0 + e2
    wt_ref[...] = jnp.where(lane == 0, 1.0 / den, jnp.where(lane == 1, e2 / den, 0.0))

    chosen = jnp.where((lane == i1) | (lane == i2), 1.0, 0.0)
    ri = lax.broadcasted_iota(jnp.int32, (tr, tr), 0)
    ci = lax.broadcasted_iota(jnp.int32, (tr, tr), 1)
    before = jnp.where(ri > ci, 1.0, 0.0).astype(BF16)
    prefix = jnp.dot(before, chosen.astype(BF16), preferred_element_type=F32) + cnt_ref[0:1, :]
    ra = jnp.sum(jnp.where(lane == i1, prefix, 0.0), axis=-1, keepdims=True).astype(jnp.int32)
    rb = jnp.sum(jnp.where(lane == i2, prefix, 0.0), axis=-1, keepdims=True).astype(jnp.int32)
    ix_ref[...] = jnp.where(lane == 0, i1, jnp.where(lane == 1, i2, jnp.where(lane == 2, ra, jnp.where(lane == 3, rb, 0))))
    cnt_ref[...] = cnt_ref[...] + jnp.sum(chosen, axis=0, keepdims=True)


def _router(h, norm_w, router_w):
    m, d = h.shape
    n_exp = router_w.shape[1]
    assert 2 <= n_exp <= LANES
    rw = jnp.pad(router_w, ((0, 0), (0, LANES - n_exp)))
    tr = _pick(m, 520, 8)
    row_spec = pl.BlockSpec((tr, d), lambda i: (i, 0))
    lane_spec = pl.BlockSpec((tr, LANES), lambda i: (i, 0))
    return pl.pallas_call(
        functools.partial(_router_kernel, n_exp=n_exp),
        grid=(m // tr,),
        in_specs=[row_spec, pl.BlockSpec((1, d), lambda i: (0, 0)), pl.BlockSpec((d, LANES), lambda i: (0, 0))],
        out_specs=[row_spec, lane_spec, lane_spec, pl.BlockSpec((8, LANES), lambda i: (0, 0))],
        out_shape=[jax.ShapeDtypeStruct((m, d), BF16), jax.ShapeDtypeStruct((m, LANES), F32),
                   jax.ShapeDtypeStruct((m, LANES), jnp.int32), jax.ShapeDtypeStruct((8, LANES), F32)],
        compiler_params=_params(("arbitrary",), 12 * tr * d * 4),
        name="router",
    )(h, norm_w.reshape(1, d), rw)


MOE_TILE = 1152
MOE_SUBS = 3


def _moe_tile_rows(m):
    return MOE_TILE if m >= 4 * MOE_TILE else 96


def _dispatch_kernel(pa_ref, pb_ref, u_ref, z_ref, xs_ref, sem):
    del z_ref
    i = pl.program_id(0)
    tr = u_ref.shape[0]

    def copy(r, pos_ref):
        return pltpu.make_async_copy(u_ref.at[r], xs_ref.at[pos_ref[i * tr + r]], sem)

    def issue(r, carry):
        copy(r, pa_ref).start()
        copy(r, pb_ref).start()
        return carry

    lax.fori_loop(0, tr, issue, 0)
    for _ in range(2):
        pltpu.make_async_copy(u_ref, xs_ref.at[pl.ds(0, tr)], sem).wait()


def _dispatch(u3, pos_a, pos_b, n_rows):
    m, s, l = u3.shape
    tr = _pick(m, 520, 8)
    return pl.pallas_call(
        _dispatch_kernel,
        grid_spec=pltpu.PrefetchScalarGridSpec(
            num_scalar_prefetch=2, grid=(m // tr,),
            in_specs=[pl.BlockSpec((tr, s, l), lambda i, pa, pb: (i, 0, 0)), pl.BlockSpec(memory_space=pl.ANY)],
            out_specs=pl.BlockSpec(memory_space=pl.ANY),
            scratch_shapes=[pltpu.SemaphoreType.DMA]),
        out_shape=jax.ShapeDtypeStruct((n_rows, s, l), u3.dtype),
        input_output_aliases={3: 0},
        compiler_params=_params(("arbitrary",), 4 * tr * s * l * 2),
        name="moe_dispatch",
    )(pos_a, pos_b, u3, jnp.zeros((n_rows, s, l), u3.dtype))


def _moe_ffn_kernel(te_ref, rows_ref, src_ref, x_ref, w1_ref, w3_ref, w2_ref, o_ref, w1b_ref, w3b_ref, w2b_ref):
    del te_ref, src_ref
    s = pl.program_id(0)
    f = pl.program_id(1)
    rows = rows_ref[s]
    sub = x_ref.shape[0] // MOE_SUBS

    def block(i):
        sl = slice(i * sub, (i + 1) * sub)
        x = x_ref[sl, :]
        h1 = jnp.dot(x, w1b_ref[...], preferred_element_type=F32)
        h3 = jnp.dot(x, w3b_ref[...], preferred_element_type=F32)
        act = (_silu(h1) * h3).astype(BF16)
        o_ref[sl, :] += jnp.dot(act, w2b_ref[...], preferred_element_type=F32)

    @pl.when(f == 0)
    def _():
        o_ref[...] = jnp.zeros_like(o_ref)

    full = rows > (MOE_SUBS - 1) * sub

    @pl.when(full)
    def _():
        x = x_ref[...]
        h1 = jnp.dot(x, w1_ref[0].astype(BF16), preferred_element_type=F32)
        h3 = jnp.dot(x, w3_ref[0].astype(BF16), preferred_element_type=F32)
        act = (_silu(h1) * h3).astype(BF16)
        o_ref[...] += jnp.dot(act, w2_ref[0].astype(BF16), preferred_element_type=F32)

    @pl.when(jnp.logical_and(rows > 0, jnp.logical_not(full)))
    def _():
        w1b_ref[...] = w1_ref[0].astype(BF16)
        w3b_ref[...] = w3_ref[0].astype(BF16)
        w2b_ref[...] = w2_ref[0].astype(BF16)
        block(0)

    for i in range(1, MOE_SUBS - 1):
        @pl.when(jnp.logical_and(rows > i * sub, jnp.logical_not(full)))
        def _():
            block(i)


def _moe_ffn(xs, w1, w3, w2, tile_expert, tile_rows, tile_src, r):
    p, d = xs.shape
    ff = w1.shape[2]
    tf = _pick(ff, 256, LANES)
    nf = ff // tf
    f_of = lambda s, f, rows: jnp.where(rows[s] > 0, f, nf - 1)
    vmem = 2 * (r * d * 2 + 3 * d * tf * 4 + r * d * 4) + 3 * d * tf * 2 + 6 * (r // MOE_SUBS) * tf * 4
    return pl.pallas_call(
        _moe_ffn_kernel,
        grid_spec=pltpu.PrefetchScalarGridSpec(
            num_scalar_prefetch=3, grid=(p // r, nf),
            in_specs=[pl.BlockSpec((r, d), lambda s, f, te, rows, src: (src[s], 0)),
                      pl.BlockSpec((1, d, tf), lambda s, f, te, rows, src: (te[s], 0, f_of(s, f, rows))),
                      pl.BlockSpec((1, d, tf), lambda s, f, te, rows, src: (te[s], 0, f_of(s, f, rows))),
                      pl.BlockSpec((1, tf, d), lambda s, f, te, rows, src: (te[s], f_of(s, f, rows), 0))],
            out_specs=pl.BlockSpec((r, d), lambda s, f, te, rows, src: (s, 0)),
            scratch_shapes=[pltpu.VMEM((d, tf), BF16), pltpu.VMEM((d, tf), BF16), pltpu.VMEM((tf, d), BF16)]),
        out_shape=jax.ShapeDtypeStruct((p, d), F32),
        compiler_params=_params(("arbitrary", "arbitrary"), vmem + (4 << 20)),
        name="moe_ffn",
    )(tile_expert, tile_rows, tile_src, xs, w1, w3, w2)


def _combine_kernel(pa_ref, pb_ref, wt_ref, ys_ref, o_ref, bufa_ref, bufb_ref, sem):
    i = pl.program_id(0)
    tr = o_ref.shape[0]

    def copy(r, pos_ref, buf_ref):
        return pltpu.make_async_copy(ys_ref.at[pl.ds(pos_ref[i * tr + r], 1), :], buf_ref.at[pl.ds(r, 1), :], sem)

    def issue(r, carry):
        copy(r, pa_ref, bufa_ref).start()
        copy(r, pb_ref, bufb_ref).start()
        return carry

    lax.fori_loop(0, tr, issue, 0)
    pltpu.make_async_copy(ys_ref.at[pl.ds(0, tr), :], bufa_ref, sem).wait()
    pltpu.make_async_copy(ys_ref.at[pl.ds(0, tr), :], bufb_ref, sem).wait()
    o_ref[...] = wt_ref[:, 0:1] * bufa_ref[...] + wt_ref[:, 1:2] * bufb_ref[...]


def _combine(ys, wt, pos_a, pos_b):
    m = wt.shape[0]
    d = ys.shape[1]
    tr = _pick(m, 520, 8)
    return pl.pallas_call(
        _combine_kernel,
        grid_spec=pltpu.PrefetchScalarGridSpec(
            num_scalar_prefetch=2, grid=(m // tr,),
            in_specs=[pl.BlockSpec((tr, LANES), lambda i, pa, pb: (i, 0)), pl.BlockSpec(memory_space=pl.ANY)],
            out_specs=pl.BlockSpec((tr, d), lambda i, pa, pb: (i, 0)),
            scratch_shapes=[pltpu.VMEM((tr, d), F32), pltpu.VMEM((tr, d), F32), pltpu.SemaphoreType.DMA]),
        out_shape=jax.ShapeDtypeStruct((m, d), F32),
        compiler_params=_params(("arbitrary",), 8 * tr * d * 4),
        name="moe_combine",
    )(pos_a, pos_b, wt, ys)


def _moe(h, norm_w, router_w, w1, w3, w2, e0):
    m, d = h.shape
    n_exp = router_w.shape[1]
    r = _moe_tile_rows(m)
    n_tiles = (2 * m + n_exp * (r - 1) + r - 1) // r
    u, wt, ix, cnt = _router(h, norm_w, router_w)

    cnt = cnt[0, :n_exp].astype(jnp.int32)
    nt = (cnt + r - 1) // r
    tend = jnp.cumsum(nt)
    tstart = tend - nt
    pos_a = tstart[ix[:, 0]] * r + ix[:, 2]
    pos_b = tstart[ix[:, 1]] * r + ix[:, 3]
    n_used = tend[n_exp - 1]
    s = jnp.arange(n_tiles, dtype=jnp.int32)
    tile_src = jnp.clip(s, 0, jnp.maximum(n_used - 1, 0))
    tile_expert = jnp.minimum(jnp.searchsorted(tend, tile_src, side="right"), n_exp - 1).astype(jnp.int32)
    tile_rows = jnp.where(s < n_used, jnp.clip(cnt[tile_expert] - (s - tstart[tile_expert]) * r, 0, r), 0).astype(jnp.int32)

    assert d % LANES == 0
    xs = _dispatch(u.reshape(m, d // LANES, LANES), pos_a, pos_b, n_tiles * r)
    ys = _moe_ffn(xs.reshape(n_tiles * r, d), w1, w3, w2, tile_expert + e0, tile_rows, tile_src, r)
    return _combine(ys, wt, pos_a, pos_b)


def kernel(x, meta_tokens, attn_norm, w_in, short_conv_w, a_log, dt_bias, dn_norm, w_dn_out, dw_conv_w, dw_conv_b,
           conv_ln_g, conv_ln_b, w_conv_out, w_merge_out, ffn_norm, dense_w1, dense_w3, dense_w2, router_w,
           moe_w1, moe_w3, moe_w2, final_norm):
    bsz, seq, d = x.shape
    n_meta = meta_tokens.shape[0]
    depth, heads = a_log.shape
    pad = (-n_meta) % CHUNK
    lp = pad + n_meta + seq
    m = bsz * lp
    hd = heads * DN_DK
    ch = dw_conv_w.shape[2]
    assert lp % CHUNK == 0 and DN_DK == DN_DV == LANES and heads <= LANES
    off_b = 3 * hd + heads * DN_DV
    off_glu = off_b + 2 * heads
    assert w_in.shape[2] == off_glu + 2 * ch + 2 * d

    w_t = jnp.swapaxes(w_in, 1, 2)
    assert off_b + LANES <= w_in.shape[2] and 2 * heads <= LANES

    meta = jnp.broadcast_to(meta_tokens[None].astype(x.dtype), (bsz, n_meta, d))
    h = jnp.concatenate([jnp.zeros((bsz, pad, d), x.dtype), meta, x], axis=1).reshape(m, d)

    delta = None
    for i in range(depth):
        h_new, u = _addnorm(h, delta, attn_norm[i], lp=lp, pad=pad, want_h=delta is not None)
        h = h if delta is None else h_new
        qkvz = _mm_nt(u, w_t, i, 0, off_b, name="in_proj_qkvz")
        ba = _mm_nt(u, w_t, i, off_b, LANES, name="in_proj_ba")
        tail = _mm_nt(u, w_t, i, off_glu, 2 * ch + 2 * d, out_dtype=BF16,
                      name="in_proj_tail")
        gate_params = jnp.zeros((8, LANES), F32)
        gate_params = gate_params.at[0, 0:heads].set(a_log[i]).at[1, 0:heads].set(dt_bias[i])

        qkvz3 = qkvz.reshape(bsz, lp, off_b)
        tail3 = tail.reshape(bsz, lp, 2 * ch + 2 * d)
        prep = _gdn_prep(qkvz3, ba.reshape(bsz, lp, LANES), short_conv_w[i], gate_params, heads=heads, pad=pad)
        o_dn = _gdn_scan(*prep, qkvz3, dn_norm[i], heads=heads)
        o_cv = _conformer(tail3, dw_conv_w[i], dw_conv_b[i], conv_ln_g[i], conv_ln_b[i])
        merged = _branches(o_dn.reshape(m, hd), o_cv.reshape(m, ch), w_dn_out, w_conv_out, i, tail, 2 * ch)
        h = _mm(merged, w_merge_out, i, res=h, name="merge_out")

        j = i // 2
        if i % 2 == 0:
            _, u = _addnorm(h, None, ffn_norm[i], lp=lp, pad=0, want_h=False)
            delta = _ffn(u, dense_w1, dense_w3, dense_w2, j)
        else:
            n_exp = moe_w1.shape[1]
            flat = lambda t: t.reshape((-1,) + t.shape[2:])
            delta = _moe(h, ffn_norm[i], router_w[j], flat(moe_w1), flat(moe_w3), flat(moe_w2), j * n_exp)

    return _final_norm(h.reshape(bsz, lp, d), delta.reshape(bsz, lp, d), final_norm, skip=pad + n_meta)
```

```python
import functools

import jax
import jax.numpy as jnp
from jax import lax
from jax.experimental import pallas as pl
from jax.experimental.pallas import tpu as pltpu

DN_DK = 128
DN_DV = 128
CHUNK = 64
EPS = 1e-6
LANES = 128
SUBLANES = 8
NEG_BIG = -1e30
V7X_VMEM_BYTES = 64 * 1024 * 1024
VMEM_CAP = 56 * 1024 * 1024

F32 = jnp.float32
BF16 = jnp.bfloat16
HI = lax.Precision.HIGHEST


def _pick(n, target, mult):
    best = None
    for d in range(mult, min(n, target) + 1, mult):
        if n % d == 0:
            best = d
    assert best is not None, (n, target, mult)
    return best


def _params(sem, vmem_bytes):
    return pltpu.CompilerParams(dimension_semantics=sem,
                                vmem_limit_bytes=int(min(VMEM_CAP, max(vmem_bytes, 16 * 1024 * 1024))))


def _sigmoid(x):
    return 0.5 * jnp.tanh(0.5 * x) + 0.5


def _silu(x):
    return x * _sigmoid(x)


def _softplus(x):
    return jnp.maximum(x, 0.0) + jnp.log(1.0 + jnp.exp(-jnp.abs(x)))


def _addnorm_kernel(*refs, lp, pad, has_delta, want_h):
    it = iter(refs)
    h_ref = next(it)
    d_ref = next(it) if has_delta else None
    w_ref = next(it)
    hn_ref = next(it) if want_h else None
    u_ref = next(it)
    tr = h_ref.shape[0]
    h = h_ref[...]
    if has_delta:
        h = h + d_ref[...]
    if want_h:
        hn_ref[...] = h
    y = h * lax.rsqrt(jnp.mean(h * h, axis=-1, keepdims=True) + EPS) * w_ref[...]
    row = pl.program_id(0) * tr + lax.broadcasted_iota(jnp.int32, (tr, 1), 0)
    y = jnp.where(row % lp >= pad, y, 0.0)
    u_ref[...] = y.astype(u_ref.dtype)


def _addnorm(h, delta, w, *, lp, pad, want_h):
    m, d = h.shape
    tr = _pick(m, 520, 8)
    has_delta = delta is not None
    row_spec = pl.BlockSpec((tr, d), lambda i: (i, 0))
    in_specs = [row_spec] + ([row_spec] if has_delta else []) + [pl.BlockSpec((1, d), lambda i: (0, 0))]
    out_shape = ([jax.ShapeDtypeStruct((m, d), F32)] if want_h else []) + [jax.ShapeDtypeStruct((m, d), BF16)]
    out_specs = ([row_spec] if want_h else []) + [row_spec]
    args = [h] + ([delta] if has_delta else []) + [w.reshape(1, d)]
    outs = pl.pallas_call(
        functools.partial(_addnorm_kernel, lp=lp, pad=pad, has_delta=has_delta, want_h=want_h),
        grid=(m // tr,), in_specs=in_specs, out_specs=out_specs, out_shape=out_shape,
        compiler_params=_params(("parallel",), 12 * tr * d * 4),
        name="addnorm",
    )(*args)
    return outs if want_h else (None, outs[0])


def _final_norm_kernel(h_ref, d_ref, w_ref, o_ref):
    h = h_ref[0] + d_ref[0]
    o_ref[0] = h * lax.rsqrt(jnp.mean(h * h, axis=-1, keepdims=True) + EPS) * w_ref[...]


def _final_norm(h, delta, w, *, skip):
    b, lp, d = h.shape
    seq = lp - skip
    tr = _pick(seq, 512, SUBLANES)
    assert skip % SUBLANES == 0
    in_spec = pl.BlockSpec((pl.Element(1), pl.Element(tr), pl.Element(d)),
                           lambda bi, i: (bi, pl.multiple_of(skip + i * tr, SUBLANES), 0))
    return pl.pallas_call(
        _final_norm_kernel,
        grid=(b, seq // tr),
        in_specs=[in_spec, in_spec, pl.BlockSpec((1, d), lambda bi, i: (0, 0))],
        out_specs=pl.BlockSpec((1, tr, d), lambda bi, i: (bi, i, 0)),
        out_shape=jax.ShapeDtypeStruct((b, seq, d), F32),
        compiler_params=_params(("parallel", "parallel"), 12 * tr * d * 4),
        name="final_norm",
    )(h, delta, w.reshape(1, d))


def _mm_kernel(x_ref, w_ref, *rest):
    o_ref = rest[-1]
    y = jnp.dot(x_ref[...], w_ref[0].astype(BF16), preferred_element_type=F32)
    if len(rest) == 2:
        y = rest[0][...] + y
    o_ref[...] = y.astype(o_ref.dtype)


def _mm(x, w, layer, *, n_out=None, res=None, tn=512, out_dtype=F32, name="mm"):
    m, k = x.shape
    n_out = w.shape[2] if n_out is None else n_out
    tn = min(tn, n_out)
    assert n_out % tn == 0
    tm = _pick(m, 2080, 16)
    vmem = 2 * (tm * k * 2 + k * tn * 4 + 2 * tm * tn * 4) + k * tn * 2 + tm * tn * 4
    out_spec = pl.BlockSpec((tm, tn), lambda i, j: (i, j))
    return pl.pallas_call(
        _mm_kernel,
        grid=(m // tm, n_out // tn),
        in_specs=[pl.BlockSpec((tm, k), lambda i, j: (i, 0)),
                  pl.BlockSpec((1, k, tn), lambda i, j: (layer, 0, j))] + ([] if res is None else [out_spec]),
        out_specs=out_spec,
        out_shape=jax.ShapeDtypeStruct((m, n_out), out_dtype),
        compiler_params=_params(("parallel", "arbitrary"), vmem + (4 << 20)),
        name=name,
    )(*((x, w) if res is None else (x, w, res)))


def _mm_nt_kernel(x_ref, wt_ref, o_ref):
    y = lax.dot_general(x_ref[...], wt_ref[0].astype(BF16), (((1,), (1,)), ((), ())), preferred_element_type=F32)
    o_ref[...] = y.astype(o_ref.dtype)


def _mm_nt(x, wt, layer, row0, n_out, *, tn=512, out_dtype=F32, name="mm_nt"):
    m, k = x.shape
    tn = min(tn, n_out)
    assert n_out % tn == 0 and row0 % SUBLANES == 0 and row0 + n_out <= wt.shape[1]
    tm = _pick(m, 2080, 16)
    vmem = 2 * (tm * k * 2 + k * tn * 4 + tm * tn * 4) + 2 * k * tn * 2 + tm * tn * 4
    return pl.pallas_call(
        _mm_nt_kernel,
        grid=(m // tm, n_out // tn),
        in_specs=[pl.BlockSpec((tm, k), lambda i, j: (i, 0)),
                  pl.BlockSpec((pl.Element(1), pl.Element(tn), pl.Element(k)), lambda i, j: (layer, pl.multiple_of(row0 + j * tn, SUBLANES), 0))],
        out_specs=pl.BlockSpec((tm, tn), lambda i, j: (i, j)),
        out_shape=jax.ShapeDtypeStruct((m, n_out), out_dtype),
        compiler_params=_params(("parallel", "arbitrary"), vmem + (4 << 20)),
        name=name,
    )(x, wt)


GDN_HEAD_GROUP = 16


def _gdn_prep_kernel(q_ref, k_ref, v_ref, qh_ref, kh_ref, vh_ref, ba_ref, cw_ref, gp_ref,
                     u_ref, w_ref, qd_ref, kd_ref, qk_ref, gc_ref, ext_ref, *, heads, pad):
    c = CHUNK
    hd = heads * DN_DK
    n = pl.program_id(1)

    def conv_silu(main_ref, halo_ref, part):
        ext_ref[0:8, :] = halo_ref[0]
        ext_ref[8:8 + c, :] = main_ref[0]
        kw = cw_ref.shape[0]
        acc = jnp.zeros((c, hd), F32)
        for j in range(kw):
            acc = acc + cw_ref[j:j + 1, part * hd:(part + 1) * hd] * ext_ref[8 - (kw - 1) + j:8 - (kw - 1) + j + c, :]
        return _silu(acc)

    row = n * c + lax.broadcasted_iota(jnp.int32, (c, 1), 0)
    live = row >= pad
    ba = ba_ref[0]
    alpha = pltpu.roll(ba, LANES - heads, axis=1)
    beta = jnp.where(live, _sigmoid(ba), 0.0)
    g = jnp.where(live, -jnp.exp(gp_ref[0:1, :]) * _softplus(alpha + gp_ref[1:2, :]), 0.0)

    ii = lax.broadcasted_iota(jnp.int32, (2 * c, c), 0)
    jj = lax.broadcasted_iota(jnp.int32, (2 * c, c), 1)
    tri_pad = jnp.where((ii >= jj) & (ii < c), 1.0, 0.0).astype(F32)
    gc_pad = jnp.dot(tri_pad, g, precision=HI, preferred_element_type=F32)
    gc = gc_pad[0:c]
    gct = gc_pad.T
    gc_ref[0] = gc
    eg = jnp.exp(gc)
    erev = jnp.exp(gc[c - 1:c, :] - gc)

    i2 = lax.broadcasted_iota(jnp.int32, (c, 2 * c), 0)
    j2 = lax.broadcasted_iota(jnp.int32, (c, 2 * c), 1)
    causal = i2 >= j2
    i1 = lax.broadcasted_iota(jnp.int32, (c, c), 0)
    j1 = lax.broadcasted_iota(jnp.int32, (c, c), 1)
    strict = i1 > j1

    qa = conv_silu(q_ref, qh_ref, 0)
    ka = conv_silu(k_ref, kh_ref, 1)
    va = conv_silu(v_ref, vh_ref, 2)

    def bdot(a, b):
        return jnp.dot(a.astype(BF16), b.astype(BF16), preferred_element_type=F32)

    for h0 in range(0, heads, GDN_HEAD_GROUP):
        group = range(h0, min(h0 + GDN_HEAD_GROUP, heads))
        a_list, rhs_list = [], []
        for h in group:
            sl = slice(h * DN_DK, (h + 1) * DN_DK)
            qh = qa[:, sl]
            kh = ka[:, sl]
            qh = qh * lax.rsqrt(jnp.sum(qh * qh, axis=-1, keepdims=True) + EPS) * (DN_DK ** -0.5)
            kh = kh * lax.rsqrt(jnp.sum(kh * kh, axis=-1, keepdims=True) + EPS)
            bh = beta[:, h:h + 1]
            egh = eg[:, h:h + 1]
            kb = kh * bh
            decay = jnp.exp(jnp.where(causal, gc[:, h:h + 1] - gct[h:h + 1, :], NEG_BIG))
            k_pad = jnp.concatenate([kh, jnp.zeros_like(kh)], axis=0)
            lhs = jnp.concatenate([kb, qh], axis=0).astype(BF16)
            kq = lax.dot_general(lhs, k_pad.astype(BF16), (((1,), (1,)), ((), ())), preferred_element_type=F32)
            a_list.append(jnp.where(strict, kq[0:c, 0:c] * decay[:, 0:c], 0.0))
            qk_ref[0, :, h * 2 * c:(h + 1) * 2 * c] = (kq[c:2 * c, :] * decay).astype(qk_ref.dtype)
            rhs_list.append(jnp.concatenate([va[:, sl] * bh, kb * egh], axis=1))
            qd_ref[0, :, sl] = (qh * egh).astype(qd_ref.dtype)
            kd_ref[0, :, sl] = (kh * erev[:, h:h + 1]).astype(kd_ref.dtype)

        m_list = [-a for a in a_list]
        p_list = a_list
        span = 2
        while span < c:
            p_list = [bdot(p, p) for p in p_list]
            m_list = [mm + p + bdot(mm, p) for mm, p in zip(m_list, p_list)]
            span *= 2

        for h, mm, rhs in zip(group, m_list, rhs_list):
            sl = slice(h * DN_DK, (h + 1) * DN_DK)
            sol = rhs + bdot(mm, rhs)
            u_ref[0, :, sl] = sol[:, 0:DN_DV]
            w_ref[0, :, sl] = sol[:, DN_DV:DN_DV + DN_DK].astype(w_ref.dtype)


def _gdn_prep(qkvz, ba, conv_w, gate_params, *, heads, pad):
    b, lp, _ = qkvz.shape
    hd = heads * DN_DK
    nc = lp // CHUNK
    main = lambda part: pl.BlockSpec((1, CHUNK, hd), lambda bi, n: (bi, n, part))
    halo = lambda part: pl.BlockSpec((1, 8, hd), lambda bi, n: (bi, jnp.maximum(n * (CHUNK // 8) - 1, 0), part))
    out_hd = pl.BlockSpec((1, CHUNK, hd), lambda bi, n: (bi, n, 0))
    shape = lambda width, dtype: jax.ShapeDtypeStruct((b, lp, width), dtype)
    return pl.pallas_call(
        functools.partial(_gdn_prep_kernel, heads=heads, pad=pad),
        grid=(b, nc),
        in_specs=[main(0), main(1), main(2), halo(0), halo(1), halo(2),
                  pl.BlockSpec((1, CHUNK, LANES), lambda bi, n: (bi, n, 0)),
                  pl.BlockSpec(conv_w.shape, lambda bi, n: (0, 0)),
                  pl.BlockSpec(gate_params.shape, lambda bi, n: (0, 0))],
        out_specs=[out_hd, out_hd, out_hd, out_hd,
                   pl.BlockSpec((1, CHUNK, heads * 2 * CHUNK), lambda bi, n: (bi, n, 0)),
                   pl.BlockSpec((1, CHUNK, LANES), lambda bi, n: (bi, n, 0))],
        out_shape=[shape(hd, F32), shape(hd, BF16), shape(hd, BF16), shape(hd, BF16),
                   shape(heads * 2 * CHUNK, BF16), shape(LANES, F32)],
        scratch_shapes=[pltpu.VMEM((8 + CHUNK, hd), F32)],
        compiler_params=_params(("parallel", "arbitrary"), 40 * CHUNK * hd * 4),
        name="gdn_prep",
    )(qkvz, qkvz, qkvz, qkvz, qkvz, qkvz, ba, conv_w, gate_params)


def _gdn_scan_kernel(u_ref, w_ref, qd_ref, kd_ref, qk_ref, gc_ref, z_ref, nw_ref, o_ref, s_ref, *, heads):
    c = CHUNK
    bsz = u_ref.shape[0]

    @pl.when(pl.program_id(0) == 0)
    def _():
        s_ref[...] = jnp.zeros_like(s_ref)

    for b in range(bsz):
        cd = jnp.exp(gc_ref[b, c - 1:c, :])
        for h in range(heads):
            sl = slice(h * DN_DV, (h + 1) * DN_DV)
            s = s_ref[b * heads + h]
            ws_qs = jnp.dot(jnp.concatenate([w_ref[b, :, sl], qd_ref[b, :, sl]], axis=0).astype(BF16), s.astype(BF16),
                            preferred_element_type=F32)
            v_new = (u_ref[b, :, sl] - ws_qs[0:c]).astype(BF16)
            o = ws_qs[c:2 * c] + jnp.dot(qk_ref[b, :, h * 2 * c:h * 2 * c + c].astype(BF16), v_new,
                                         preferred_element_type=F32)
            s_ref[b * heads + h] = s * cd[:, h:h + 1] + lax.dot_general(
                kd_ref[b, :, sl].astype(BF16), v_new, (((0,), (0,)), ((), ())), preferred_element_type=F32)
            on = o * lax.rsqrt(jnp.mean(o * o, axis=-1, keepdims=True) + EPS) * nw_ref[...]
            o_ref[b, :, sl] = (on * _silu(z_ref[b, :, sl])).astype(o_ref.dtype)


def _gdn_scan(u, w, qd, kd, qk, gc, qkvz, norm_w, *, heads):
    b, lp, hd = u.shape
    nc = lp // CHUNK
    blk = lambda width, col: pl.BlockSpec((b, CHUNK, width), lambda n: (0, n, col))
    return pl.pallas_call(
        functools.partial(_gdn_scan_kernel, heads=heads),
        grid=(nc,),
        in_specs=[blk(hd, 0), blk(hd, 0), blk(hd, 0), blk(hd, 0), blk(heads * 2 * CHUNK, 0), blk(LANES, 0),
                  blk(hd, 3), pl.BlockSpec((1, DN_DV), lambda n: (0, 0))],
        out_specs=blk(hd, 0),
        out_shape=jax.ShapeDtypeStruct((b, lp, hd), BF16),
        scratch_shapes=[pltpu.VMEM((b * heads, DN_DK, DN_DV), F32)],
        compiler_params=_params(("arbitrary",), 24 * b * CHUNK * hd * 4),
        name="gdn_scan",
    )(u, w, qd, kd, qk, gc, qkvz, norm_w.reshape(1, DN_DV))


def _gdn_pipe_kernel(q_ref, k_ref, v_ref, qh_ref, kh_ref, vh_ref, ba_ref, cw_ref, gp_ref, z_ref, nw_ref,
                     o_ref, s_ref, ext_ref, pu_ref, pw_ref, pqd_ref, pkd_ref, pqk_ref, pcd_ref, *, heads, pad, nc):
    c = CHUNK
    hd = heads * DN_DK
    n = pl.program_id(1)
    slot = n % 2
    prev = 1 - slot
    chunk = jnp.minimum(n, nc - 1)

    @pl.when(n == 0)
    def _():
        for ref in (s_ref, pu_ref, pw_ref, pqd_ref, pkd_ref, pqk_ref, pcd_ref):
            ref[...] = jnp.zeros_like(ref)

    cd = pcd_ref[prev][0:1, :]
    for h in range(heads):
        sl = slice(h * DN_DV, (h + 1) * DN_DV)
        s = s_ref[h]
        ws_qs = jnp.dot(jnp.concatenate([pw_ref[prev, :, sl], pqd_ref[prev, :, sl]], axis=0).astype(BF16), s.astype(BF16),
                        preferred_element_type=F32)
        v_new = (pu_ref[prev, :, sl] - ws_qs[0:c]).astype(BF16)
        o = ws_qs[c:2 * c] + jnp.dot(pqk_ref[prev, :, h * 2 * c:h * 2 * c + c].astype(BF16), v_new,
                                     preferred_element_type=F32)
        s_ref[h] = s * cd[:, h:h + 1] + lax.dot_general(
            pkd_ref[prev, :, sl].astype(BF16), v_new, (((0,), (0,)), ((), ())), preferred_element_type=F32)
        on = o * lax.rsqrt(jnp.mean(o * o, axis=-1, keepdims=True) + EPS) * nw_ref[...]
        o_ref[0, :, sl] = (on * _silu(z_ref[0, :, sl])).astype(o_ref.dtype)

    def conv_silu(main_ref, halo_ref, part):
        ext_ref[0:8, :] = halo_ref[0]
        ext_ref[8:8 + c, :] = main_ref[0]
        kw = cw_ref.shape[0]
        acc = jnp.zeros((c, hd), F32)
        for j in range(kw):
            acc = acc + cw_ref[j:j + 1, part * hd:(part + 1) * hd] * ext_ref[8 - (kw - 1) + j:8 - (kw - 1) + j + c, :]
        return _silu(acc)

    row = chunk * c + lax.broadcasted_iota(jnp.int32, (c, 1), 0)
    live = row >= pad
    ba = ba_ref[0]
    alpha = pltpu.roll(ba, LANES - heads, axis=1)
    beta = jnp.where(live, _sigmoid(ba), 0.0)
    g = jnp.where(live, -jnp.exp(gp_ref[0:1, :]) * _softplus(alpha + gp_ref[1:2, :]), 0.0)
    ii = lax.broadcasted_iota(jnp.int32, (2 * c, c), 0)
    jj = lax.broadcasted_iota(jnp.int32, (2 * c, c), 1)
    tri_pad = jnp.where((ii >= jj) & (ii < c), 1.0, 0.0).astype(F32)
    gc_pad = jnp.dot(tri_pad, g, precision=HI, preferred_element_type=F32)
    gc = gc_pad[0:c]
    gct = gc_pad.T
    eg = jnp.exp(gc)
    erev = jnp.exp(gc[c - 1:c, :] - gc)
    pcd_ref[slot] = jnp.broadcast_to(jnp.exp(gc[c - 1:c, :]), (SUBLANES, LANES))

    i2 = lax.broadcasted_iota(jnp.int32, (c, 2 * c), 0)
    j2 = lax.broadcasted_iota(jnp.int32, (c, 2 * c), 1)
    causal = i2 >= j2
    i1 = lax.broadcasted_iota(jnp.int32, (c, c), 0)
    j1 = lax.broadcasted_iota(jnp.int32, (c, c), 1)
    strict = i1 > j1

    qa = conv_silu(q_ref, qh_ref, 0)
    ka = conv_silu(k_ref, kh_ref, 1)
    va = conv_silu(v_ref, vh_ref, 2)

    def bdot(a, b):
        return jnp.dot(a.astype(BF16), b.astype(BF16), preferred_element_type=F32)

    a_list, rhs_list = [], []
    for h in range(heads):
        sl = slice(h * DN_DK, (h + 1) * DN_DK)
        qh = qa[:, sl]
        kh = ka[:, sl]
        qh = qh * lax.rsqrt(jnp.sum(qh * qh, axis=-1, keepdims=True) + EPS) * (DN_DK ** -0.5)
        kh = kh * lax.rsqrt(jnp.sum(kh * kh, axis=-1, keepdims=True) + EPS)
        bh = beta[:, h:h + 1]
        egh = eg[:, h:h + 1]
        kb = kh * bh
        decay = jnp.exp(jnp.where(causal, gc[:, h:h + 1] - gct[h:h + 1, :], NEG_BIG))
        k_pad = jnp.concatenate([kh, jnp.zeros_like(kh)], axis=0)
        lhs = jnp.concatenate([kb, qh], axis=0).astype(BF16)
        kq = lax.dot_general(lhs, k_pad.astype(BF16), (((1,), (1,)), ((), ())), preferred_element_type=F32)
        a_list.append(jnp.where(strict, kq[0:c, 0:c] * decay[:, 0:c], 0.0))
        pqk_ref[slot, :, h * 2 * c:(h + 1) * 2 * c] = kq[c:2 * c, :] * decay
        rhs_list.append(jnp.concatenate([va[:, sl] * bh, kb * egh], axis=1))
        pqd_ref[slot, :, sl] = qh * egh
        pkd_ref[slot, :, sl] = kh * erev[:, h:h + 1]

    m_list = [-a for a in a_list]
    p_list = a_list
    span = 2
    while span < c:
        p_list = [bdot(p, p) for p in p_list]
        m_list = [mm + p + bdot(mm, p) for mm, p in zip(m_list, p_list)]
        span *= 2
    for h in range(heads):
        sl = slice(h * DN_DK, (h + 1) * DN_DK)
        sol = rhs_list[h] + bdot(m_list[h], rhs_list[h])
        pu_ref[slot, :, sl] = sol[:, 0:DN_DV]
        pw_ref[slot, :, sl] = sol[:, DN_DV:DN_DV + DN_DK]


def _gdn_pipe(qkvz, ba, conv_w, gate_params, norm_w, *, heads, pad):
    b, lp, _ = qkvz.shape
    hd = heads * DN_DK
    nc = lp // CHUNK
    cur = lambda n: jnp.minimum(n, nc - 1)
    main = lambda part: pl.BlockSpec((1, CHUNK, hd), lambda bi, n: (bi, cur(n), part))
    halo = lambda part: pl.BlockSpec((1, 8, hd), lambda bi, n: (bi, jnp.maximum(cur(n) * (CHUNK // 8) - 1, 0), part))
    lag = lambda width, col: pl.BlockSpec((1, CHUNK, width), lambda bi, n: (bi, jnp.maximum(n - 1, 0), col))
    slots = lambda width: pltpu.VMEM((2, CHUNK, width), F32)
    return pl.pallas_call(
        functools.partial(_gdn_pipe_kernel, heads=heads, pad=pad, nc=nc),
        grid=(b, nc + 1),
        in_specs=[main(0), main(1), main(2), halo(0), halo(1), halo(2),
                  pl.BlockSpec((1, CHUNK, LANES), lambda bi, n: (bi, cur(n), 0)),
                  pl.BlockSpec(conv_w.shape, lambda bi, n: (0, 0)),
                  pl.BlockSpec(gate_params.shape, lambda bi, n: (0, 0)),
                  lag(hd, 3), pl.BlockSpec((1, DN_DV), lambda bi, n: (0, 0))],
        out_specs=lag(hd, 0),
        out_shape=jax.ShapeDtypeStruct((b, lp, hd), BF16),
        scratch_shapes=[pltpu.VMEM((heads, DN_DK, DN_DV), F32), pltpu.VMEM((8 + CHUNK, hd), F32),
                        slots(hd), slots(hd), slots(hd), slots(hd), slots(heads * 2 * CHUNK),
                        pltpu.VMEM((2, SUBLANES, LANES), F32)],
        compiler_params=_params(("parallel", "arbitrary"), 48 * CHUNK * hd * 4),
        name="gdn_pipe",
    )(qkvz, qkvz, qkvz, qkvz, qkvz, qkvz, ba, conv_w, gate_params, qkvz, norm_w.reshape(1, DN_DV))


CONV_HALO = 32
CONV_CT = LANES


def _conformer_kernel(a_ref, g_ref, ah_ref, gh_ref, cw_ref, cb_ref, lg_ref, lb_ref, o_ref, ext_ref, sh_ref, y_ref):
    r, ch = o_ref.shape[1], o_ref.shape[2]
    kw = cw_ref.shape[0]
    ext_ref[0:CONV_HALO, :] = ah_ref[0].astype(F32) * _sigmoid(gh_ref[0].astype(F32))
    ext_ref[CONV_HALO:CONV_HALO + r, :] = a_ref[0].astype(F32) * _sigmoid(g_ref[0].astype(F32))
    base = CONV_HALO - (kw - 1)
    span = sh_ref.shape[1]

    def ch_body(ci, carry):
        c0 = pl.multiple_of(ci * CONV_CT, CONV_CT)
        for s in range(1, SUBLANES):
            sh_ref[s - 1] = ext_ref[s:s + span, pl.ds(c0, CONV_CT)]
        acc = jnp.broadcast_to(cb_ref[:, pl.ds(c0, CONV_CT)], (r, CONV_CT))
        for j in range(kw):
            q, s = divmod(base + j, SUBLANES)
            rows = slice(q * SUBLANES, q * SUBLANES + r)
            tap = sh_ref[s - 1, rows, :] if s else ext_ref[rows, pl.ds(c0, CONV_CT)]
            acc = acc + cw_ref[j:j + 1, pl.ds(c0, CONV_CT)] * tap
        y_ref[:, pl.ds(c0, CONV_CT)] = acc
        return carry

    lax.fori_loop(0, ch // CONV_CT, ch_body, 0)
    y = y_ref[...]
    mu = jnp.mean(y, axis=-1, keepdims=True)
    yc = y - mu
    var = jnp.mean(yc * yc, axis=-1, keepdims=True)
    yn = yc * lax.rsqrt(var + EPS) * lg_ref[...] + lb_ref[...]
    o_ref[0] = _silu(yn).astype(o_ref.dtype)


def _conformer(tail, conv_w, conv_b, ln_g, ln_b):
    b, lp, _ = tail.shape
    ch = conv_w.shape[1]
    assert conv_w.shape[0] - 1 <= CONV_HALO and ch % CONV_CT == 0
    r = _pick(lp, 320, CONV_HALO)
    main = lambda col: pl.BlockSpec((1, r, ch), lambda bi, i: (bi, i, col))
    halo = lambda col: pl.BlockSpec((1, CONV_HALO, ch), lambda bi, i: (bi, jnp.maximum(i * (r // CONV_HALO) - 1, 0), col))
    vec = pl.BlockSpec((1, ch), lambda bi, i: (0, 0))
    return pl.pallas_call(
        _conformer_kernel,
        grid=(b, lp // r),
        in_specs=[main(0), main(1), halo(0), halo(1), pl.BlockSpec(conv_w.shape, lambda bi, i: (0, 0)), vec, vec, vec],
        out_specs=pl.BlockSpec((1, r, ch), lambda bi, i: (bi, i, 0)),
        out_shape=jax.ShapeDtypeStruct((b, lp, ch), BF16),
        scratch_shapes=[pltpu.VMEM((CONV_HALO + r, ch), F32),
                        pltpu.VMEM((SUBLANES - 1, CONV_HALO + r - SUBLANES, CONV_CT), F32),
                        pltpu.VMEM((r, ch), F32)],
        compiler_params=_params(("parallel", "arbitrary"), 14 * r * ch * 4),
        name="conformer",
    )(tail, tail, tail, tail, conv_w, conv_b.reshape(1, ch), ln_g.reshape(1, ch), ln_b.reshape(1, ch))


def _branch_kernel(od_ref, oc_ref, wd_ref, wc_ref, gd_ref, gc_ref, o_ref):
    yd = jnp.dot(od_ref[...], wd_ref[0].astype(BF16), preferred_element_type=F32)
    yc = jnp.dot(oc_ref[...], wc_ref[0].astype(BF16), preferred_element_type=F32)
    o_ref[...] = (_sigmoid(gd_ref[...].astype(F32)) * yd + _sigmoid(gc_ref[...].astype(F32)) * yc).astype(o_ref.dtype)


def _branches(o_dn, o_cv, w_dn_out, w_conv_out, layer, tail, gate_col0):
    m, k = o_dn.shape
    d = w_dn_out.shape[2]
    assert o_cv.shape == o_dn.shape and w_conv_out.shape == w_dn_out.shape
    tn = min(512, d)
    tm = _pick(m, 1040, 16)
    g0 = gate_col0 // tn
    assert gate_col0 % tn == 0 and d % tn == 0
    x_spec = pl.BlockSpec((tm, k), lambda i, j: (i, 0))
    w_spec = pl.BlockSpec((1, k, tn), lambda i, j: (layer, 0, j))
    vmem = 2 * (2 * tm * k * 2 + 2 * k * tn * 4 + 2 * tm * tn * 4 + tm * tn * 2) + 2 * k * tn * 2 + 4 * tm * tn * 4
    return pl.pallas_call(
        _branch_kernel,
        grid=(m // tm, d // tn),
        in_specs=[x_spec, x_spec, w_spec, w_spec,
                  pl.BlockSpec((tm, tn), lambda i, j: (i, g0 + j)),
                  pl.BlockSpec((tm, tn), lambda i, j: (i, g0 + d // tn + j))],
        out_specs=pl.BlockSpec((tm, tn), lambda i, j: (i, j)),
        out_shape=jax.ShapeDtypeStruct((m, d), BF16),
        compiler_params=_params(("parallel", "arbitrary"), vmem + (4 << 20)),
        name="branches",
    )(o_dn, o_cv, w_dn_out, w_conv_out, tail, tail)


def _ffn_kernel(x_ref, w1_ref, w3_ref, w2_ref, o_ref):
    @pl.when(pl.program_id(1) == 0)
    def _():
        o_ref[...] = jnp.zeros_like(o_ref)

    x = x_ref[...]
    h1 = jnp.dot(x, w1_ref[0].astype(BF16), preferred_element_type=F32)
    h3 = jnp.dot(x, w3_ref[0].astype(BF16), preferred_element_type=F32)
    act = (_silu(h1) * h3).astype(BF16)
    o_ref[...] += jnp.dot(act, w2_ref[0].astype(BF16), preferred_element_type=F32)


def _ffn(x, w1, w3, w2, layer):
    m, d = x.shape
    ff = w1.shape[2]
    tm = _pick(m, 1040, 16)
    tf = _pick(ff, 256, LANES)
    vmem = 2 * (tm * d * 2 + 3 * d * tf * 4 + tm * d * 4) + 3 * d * tf * 2 + 4 * tm * tf * 4 + tm * d * 4
    return pl.pallas_call(
        _ffn_kernel,
        grid=(m // tm, ff // tf),
        in_specs=[pl.BlockSpec((tm, d), lambda i, f: (i, 0)),
                  pl.BlockSpec((1, d, tf), lambda i, f: (layer, 0, f)),
                  pl.BlockSpec((1, d, tf), lambda i, f: (layer, 0, f)),
                  pl.BlockSpec((1, tf, d), lambda i, f: (layer, f, 0))],
        out_specs=pl.BlockSpec((tm, d), lambda i, f: (i, 0)),
        out_shape=jax.ShapeDtypeStruct((m, d), F32),
        compiler_params=_params(("parallel", "arbitrary"), vmem + (4 << 20)),
        name="ffn",
    )(x, w1, w3, w2)


def _router_kernel(h_ref, nw_ref, rw_ref, u_ref, wt_ref, ix_ref, cnt_ref, *, n_exp):
    @pl.when(pl.program_id(0) == 0)
    def _():
        cnt_ref[...] = jnp.zeros_like(cnt_ref)

    h = h_ref[...]
    tr = h.shape[0]
    u = h * lax.rsqrt(jnp.mean(h * h, axis=-1, keepdims=True) + EPS) * nw_ref[...]
    u_ref[...] = u.astype(u_ref.dtype)
    logits = jnp.dot(u, rw_ref[...], precision=HI, preferred_element_type=F32)
    lane = lax.broadcasted_iota(jnp.int32, logits.shape, 1)
    logits = jnp.where(lane < n_exp, logits, NEG_BIG)
    m1 = jnp.max(logits, axis=-1, keepdims=True)
    i1 = jnp.min(jnp.where(logits == m1, lane, LANES), axis=-1, keepdims=True)
    rest = jnp.where(lane == i1, NEG_BIG, logits)
    m2 = jnp.max(rest, axis=-1, keepdims=True)
    i2 = jnp.min(jnp.where(rest == m2, lane, LANES), axis=-1, keepdims=True)
    e2 = jnp.exp(m2 - m1)
    den = 1.0 + e2
    wt_ref[...] = jnp.where(lane == 0, 1.0 / den, jnp.where(lane == 1, e2 / den, 0.0))

    chosen = jnp.where((lane == i1) | (lane == i2), 1.0, 0.0)
    ri = lax.broadcasted_iota(jnp.int32, (tr, tr), 0)
    ci = lax.broadcasted_iota(jnp.int32, (tr, tr), 1)
    before = jnp.where(ri > ci, 1.0, 0.0).astype(BF16)
    prefix = jnp.dot(before, chosen.astype(BF16), preferred_element_type=F32) + cnt_ref[0:1, :]
    ra = jnp.sum(jnp.where(lane == i1, prefix, 0.0), axis=-1, keepdims=True).astype(jnp.int32)
    rb = jnp.sum(jnp.where(lane == i2, prefix, 0.0), axis=-1, keepdims=True).astype(jnp.int32)
    ix_ref[...] = jnp.where(lane == 0, i1, jnp.where(lane == 1, i2, jnp.where(lane == 2, ra, jnp.where(lane == 3, rb, 0))))
    cnt_ref[...] = cnt_ref[...] + jnp.sum(chosen, axis=0, keepdims=True)


def _router(h, norm_w, router_w):
    m, d = h.shape
    n_exp = router_w.shape[1]
    assert 2 <= n_exp <= LANES
    rw = jnp.pad(router_w, ((0, 0), (0, LANES - n_exp)))
    tr = _pick(m, 520, 8)
    row_spec = pl.BlockSpec((tr, d), lambda i: (i, 0))
    lane_spec = pl.BlockSpec((tr, LANES), lambda i: (i, 0))
    return pl.pallas_call(
        functools.partial(_router_kernel, n_exp=n_exp),
        grid=(m // tr,),
        in_specs=[row_spec, pl.BlockSpec((1, d), lambda i: (0, 0)), pl.BlockSpec((d, LANES), lambda i: (0, 0))],
        out_specs=[row_spec, lane_spec, lane_spec, pl.BlockSpec((8, LANES), lambda i: (0, 0))],
        out_shape=[jax.ShapeDtypeStruct((m, d), BF16), jax.ShapeDtypeStruct((m, LANES), F32),
                   jax.ShapeDtypeStruct((m, LANES), jnp.int32), jax.ShapeDtypeStruct((8, LANES), F32)],
        compiler_params=_params(("arbitrary",), 12 * tr * d * 4),
        name="router",
    )(h, norm_w.reshape(1, d), rw)


MOE_TILE = 1152
MOE_SUBS = 3


def _moe_tile_rows(m):
    return MOE_TILE if m >= 4 * MOE_TILE else 96


def _dispatch_kernel(pa_ref, pb_ref, u_ref, z_ref, xs_ref, sem):
    del z_ref
    i = pl.program_id(0)
    tr = u_ref.shape[0]

    def copy(r, pos_ref):
        return pltpu.make_async_copy(u_ref.at[r], xs_ref.at[pos_ref[i * tr + r]], sem)

    def issue(r, carry):
        copy(r, pa_ref).start()
        copy(r, pb_ref).start()
        return carry

    lax.fori_loop(0, tr, issue, 0)
    for _ in range(2):
        pltpu.make_async_copy(u_ref, xs_ref.at[pl.ds(0, tr)], sem).wait()


def _dispatch(u3, pos_a, pos_b, n_rows):
    m, s, l = u3.shape
    tr = _pick(m, 520, 8)
    return pl.pallas_call(
        _dispatch_kernel,
        grid_spec=pltpu.PrefetchScalarGridSpec(
            num_scalar_prefetch=2, grid=(m // tr,),
            in_specs=[pl.BlockSpec((tr, s, l), lambda i, pa, pb: (i, 0, 0)), pl.BlockSpec(memory_space=pl.ANY)],
            out_specs=pl.BlockSpec(memory_space=pl.ANY),
            scratch_shapes=[pltpu.SemaphoreType.DMA]),
        out_shape=jax.ShapeDtypeStruct((n_rows, s, l), u3.dtype),
        input_output_aliases={3: 0},
        compiler_params=_params(("arbitrary",), 4 * tr * s * l * 2),
        name="moe_dispatch",
    )(pos_a, pos_b, u3, jnp.zeros((n_rows, s, l), u3.dtype))


def _moe_ffn_kernel(te_ref, rows_ref, src_ref, x_ref, w1_ref, w3_ref, w2_ref, o_ref, w1b_ref, w3b_ref, w2b_ref):
    del te_ref, src_ref
    s = pl.program_id(0)
    f = pl.program_id(1)
    rows = rows_ref[s]
    sub = x_ref.shape[0] // MOE_SUBS

    def block(i):
        sl = slice(i * sub, (i + 1) * sub)
        x = x_ref[sl, :]
        h1 = jnp.dot(x, w1b_ref[...], preferred_element_type=F32)
        h3 = jnp.dot(x, w3b_ref[...], preferred_element_type=F32)
        act = (_silu(h1) * h3).astype(BF16)
        o_ref[sl, :] += jnp.dot(act, w2b_ref[...], preferred_element_type=F32)

    @pl.when(f == 0)
    def _():
        o_ref[...] = jnp.zeros_like(o_ref)

    full = rows > (MOE_SUBS - 1) * sub

    @pl.when(full)
    def _():
        x = x_ref[...]
        h1 = jnp.dot(x, w1_ref[0].astype(BF16), preferred_element_type=F32)
        h3 = jnp.dot(x, w3_ref[0].astype(BF16), preferred_element_type=F32)
        act = (_silu(h1) * h3).astype(BF16)
        o_ref[...] += jnp.dot(act, w2_ref[0].astype(BF16), preferred_element_type=F32)

    @pl.when(jnp.logical_and(rows > 0, jnp.logical_not(full)))
    def _():
        w1b_ref[...] = w1_ref[0].astype(BF16)
        w3b_ref[...] = w3_ref[0].astype(BF16)
        w2b_ref[...] = w2_ref[0].astype(BF16)
        block(0)

    for i in range(1, MOE_SUBS - 1):
        @pl.when(jnp.logical_and(rows > i * sub, jnp.logical_not(full)))
        def _():
            block(i)


def _moe_ffn(xs, w1, w3, w2, tile_expert, tile_rows, tile_src, r):
    p, d = xs.shape
    ff = w1.shape[2]
    tf = _pick(ff, 256, LANES)
    nf = ff // tf
    f_of = lambda s, f, rows: jnp.where(rows[s] > 0, f, nf - 1)
    vmem = 2 * (r * d * 2 + 3 * d * tf * 4 + r * d * 4) + 3 * d * tf * 2 + 6 * (r // MOE_SUBS) * tf * 4
    return pl.pallas_call(
        _moe_ffn_kernel,
        grid_spec=pltpu.PrefetchScalarGridSpec(
            num_scalar_prefetch=3, grid=(p // r, nf),
            in_specs=[pl.BlockSpec((r, d), lambda s, f, te, rows, src: (src[s], 0)),
                      pl.BlockSpec((1, d, tf), lambda s, f, te, rows, src: (te[s], 0, f_of(s, f, rows))),
                      pl.BlockSpec((1, d, tf), lambda s, f, te, rows, src: (te[s], 0, f_of(s, f, rows))),
                      pl.BlockSpec((1, tf, d), lambda s, f, te, rows, src: (te[s], f_of(s, f, rows), 0))],
            out_specs=pl.BlockSpec((r, d), lambda s, f, te, rows, src: (s, 0)),
            scratch_shapes=[pltpu.VMEM((d, tf), BF16), pltpu.VMEM((d, tf), BF16), pltpu.VMEM((tf, d), BF16)]),
        out_shape=jax.ShapeDtypeStruct((p, d), F32),
        compiler_params=_params(("arbitrary", "arbitrary"), vmem + (4 << 20)),
        name="moe_ffn",
    )(tile_expert, tile_rows, tile_src, xs, w1, w3, w2)


def _combine_kernel(pa_ref, pb_ref, wt_ref, ys_ref, o_ref, bufa_ref, bufb_ref, sem):
    i = pl.program_id(0)
    tr = o_ref.shape[0]

    def copy(r, pos_ref, buf_ref):
        return pltpu.make_async_copy(ys_ref.at[pl.ds(pos_ref[i * tr + r], 1), :], buf_ref.at[pl.ds(r, 1), :], sem)

    def issue(r, carry):
        copy(r, pa_ref, bufa_ref).start()
        copy(r, pb_ref, bufb_ref).start()
        return carry

    lax.fori_loop(0, tr, issue, 0)
    pltpu.make_async_copy(ys_ref.at[pl.ds(0, tr), :], bufa_ref, sem).wait()
    pltpu.make_async_copy(ys_ref.at[pl.ds(0, tr), :], bufb_ref, sem).wait()
    o_ref[...] = wt_ref[:, 0:1] * bufa_ref[...] + wt_ref[:, 1:2] * bufb_ref[...]


def _combine(ys, wt, pos_a, pos_b):
    m = wt.shape[0]
    d = ys.shape[1]
    tr = _pick(m, 520, 8)
    return pl.pallas_call(
        _combine_kernel,
        grid_spec=pltpu.PrefetchScalarGridSpec(
            num_scalar_prefetch=2, grid=(m // tr,),
            in_specs=[pl.BlockSpec((tr, LANES), lambda i, pa, pb: (i, 0)), pl.BlockSpec(memory_space=pl.ANY)],
            out_specs=pl.BlockSpec((tr, d), lambda i, pa, pb: (i, 0)),
            scratch_shapes=[pltpu.VMEM((tr, d), F32), pltpu.VMEM((tr, d), F32), pltpu.SemaphoreType.DMA]),
        out_shape=jax.ShapeDtypeStruct((m, d), F32),
        compiler_params=_params(("arbitrary",), 8 * tr * d * 4),
        name="moe_combine",
    )(pos_a, pos_b, wt, ys)


def _moe(h, norm_w, router_w, w1, w3, w2, e0):
    m, d = h.shape
    n_exp = router_w.shape[1]
    r = _moe_tile_rows(m)
    n_tiles = (2 * m + n_exp * (r - 1) + r - 1) // r
    u, wt, ix, cnt = _router(h, norm_w, router_w)

    cnt = cnt[0, :n_exp].astype(jnp.int32)
    nt = (cnt + r - 1) // r
    tend = jnp.cumsum(nt)
    tstart = tend - nt
    pos_a = tstart[ix[:, 0]] * r + ix[:, 2]
    pos_b = tstart[ix[:, 1]] * r + ix[:, 3]
    n_used = tend[n_exp - 1]
    s = jnp.arange(n_tiles, dtype=jnp.int32)
    tile_src = jnp.clip(s, 0, jnp.maximum(n_used - 1, 0))
    tile_expert = jnp.minimum(jnp.searchsorted(tend, tile_src, side="right"), n_exp - 1).astype(jnp.int32)
    tile_rows = jnp.where(s < n_used, jnp.clip(cnt[tile_expert] - (s - tstart[tile_expert]) * r, 0, r), 0).astype(jnp.int32)

    assert d % LANES == 0
    xs = _dispatch(u.reshape(m, d // LANES, LANES), pos_a, pos_b, n_tiles * r)
    ys = _moe_ffn(xs.reshape(n_tiles * r, d), w1, w3, w2, tile_expert + e0, tile_rows, tile_src, r)
    return _combine(ys, wt, pos_a, pos_b)


def kernel(x, meta_tokens, attn_norm, w_in, short_conv_w, a_log, dt_bias, dn_norm, w_dn_out, dw_conv_w, dw_conv_b,
           conv_ln_g, conv_ln_b, w_conv_out, w_merge_out, ffn_norm, dense_w1, dense_w3, dense_w2, router_w,
           moe_w1, moe_w3, moe_w2, final_norm):
    bsz, seq, d = x.shape
    n_meta = meta_tokens.shape[0]
    depth, heads = a_log.shape
    pad = (-n_meta) % CHUNK
    lp = pad + n_meta + seq
    m = bsz * lp
    hd = heads * DN_DK
    ch = dw_conv_w.shape[2]
    assert lp % CHUNK == 0 and DN_DK == DN_DV == LANES and heads <= LANES
    off_b = 3 * hd + heads * DN_DV
    off_glu = off_b + 2 * heads
    assert w_in.shape[2] == off_glu + 2 * ch + 2 * d

    w_t = jnp.swapaxes(w_in, 1, 2)
    assert off_b + LANES <= w_in.shape[2] and 2 * heads <= LANES

    meta = jnp.broadcast_to(meta_tokens[None].astype(x.dtype), (bsz, n_meta, d))
    h = jnp.concatenate([jnp.zeros((bsz, pad, d), x.dtype), meta, x], axis=1).reshape(m, d)

    delta = None
    for i in range(depth):
        h_new, u = _addnorm(h, delta, attn_norm[i], lp=lp, pad=pad, want_h=delta is not None)
        h = h if delta is None else h_new
        qkvz = _mm_nt(u, w_t, i, 0, off_b, name="in_proj_qkvz")
        ba = _mm_nt(u, w_t, i, off_b, LANES, name="in_proj_ba")
        tail = _mm_nt(u, w_t, i, off_glu, 2 * ch + 2 * d, out_dtype=BF16,
                      name="in_proj_tail")
        gate_params = jnp.zeros((8, LANES), F32)
        gate_params = gate_params.at[0, 0:heads].set(a_log[i]).at[1, 0:heads].set(dt_bias[i])

        qkvz3 = qkvz.reshape(bsz, lp, off_b)
        tail3 = tail.reshape(bsz, lp, 2 * ch + 2 * d)
        o_dn = _gdn_pipe(qkvz3, ba.reshape(bsz, lp, LANES), short_conv_w[i], gate_params, dn_norm[i], heads=heads, pad=pad)
        o_cv = _conformer(tail3, dw_conv_w[i], dw_conv_b[i], conv_ln_g[i], conv_ln_b[i])
        merged = _branches(o_dn.reshape(m, hd), o_cv.reshape(m, ch), w_dn_out, w_conv_out, i, tail, 2 * ch)
        h = _mm(merged, w_merge_out, i, res=h, name="merge_out")

        j = i // 2
        if i % 2 == 0:
            _, u = _addnorm(h, None, ffn_norm[i], lp=lp, pad=0, want_h=False)
            delta = _ffn(u, dense_w1, dense_w3, dense_w2, j)
        else:
            n_exp = moe_w1.shape[1]
            flat = lambda t: t.reshape((-1,) + t.shape[2:])
            delta = _moe(h, ffn_norm[i], router_w[j], flat(moe_w1), flat(moe_w3), flat(moe_w2), j * n_exp)

    return _final_norm(h.reshape(bsz, lp, d), delta.reshape(bsz, lp, d), final_norm, skip=pad + n_meta)
```

```python
import functools

import jax
import jax.numpy as jnp
from jax import lax
from jax.experimental import pallas as pl
from jax.experimental.pallas import tpu as pltpu

DN_DK = 128
DN_DV = 128
CHUNK = 64
EPS = 1e-6
LANES = 128
SUBLANES = 8
NEG_BIG = -1e30
V7X_VMEM_BYTES = 64 * 1024 * 1024
VMEM_CAP = V7X_VMEM_BYTES - 8 * 1024 * 1024

F32 = jnp.float32
BF16 = jnp.bfloat16
HI = lax.Precision.HIGHEST


def _pick(n, target, mult):
    best = None
    for d in range(mult, min(n, target) + 1, mult):
        if n % d == 0:
            best = d
    assert best is not None, (n, target, mult)
    return best


def _params(sem, vmem_bytes):
    return pltpu.CompilerParams(dimension_semantics=sem,
                                vmem_limit_bytes=int(min(VMEM_CAP, max(vmem_bytes, 16 * 1024 * 1024))))


def _sigmoid(x):
    return 0.5 * jnp.tanh(0.5 * x) + 0.5


def _silu(x):
    return x * _sigmoid(x)


def _softplus(x):
    return jnp.maximum(x, 0.0) + jnp.log(1.0 + jnp.exp(-jnp.abs(x)))


def _addnorm_kernel(*refs, lp, pad, has_delta, want_h):
    it = iter(refs)
    h_ref = next(it)
    d_ref = next(it) if has_delta else None
    w_ref = next(it)
    hn_ref = next(it) if want_h else None
    u_ref = next(it)
    tr = h_ref.shape[0]
    h = h_ref[...]
    if has_delta:
        h = h + d_ref[...]
    if want_h:
        hn_ref[...] = h
    y = h * lax.rsqrt(jnp.mean(h * h, axis=-1, keepdims=True) + EPS) * w_ref[...]
    row = pl.program_id(0) * tr + lax.broadcasted_iota(jnp.int32, (tr, 1), 0)
    y = jnp.where(row % lp >= pad, y, 0.0)
    u_ref[...] = y.astype(u_ref.dtype)


def _addnorm(h, delta, w, *, lp, pad, want_h):
    m, d = h.shape
    tr = _pick(m, 520, 8)
    has_delta = delta is not None
    row_spec = pl.BlockSpec((tr, d), lambda i: (i, 0))
    in_specs = [row_spec] + ([row_spec] if has_delta else []) + [pl.BlockSpec((1, d), lambda i: (0, 0))]
    out_shape = ([jax.ShapeDtypeStruct((m, d), F32)] if want_h else []) + [jax.ShapeDtypeStruct((m, d), BF16)]
    out_specs = ([row_spec] if want_h else []) + [row_spec]
    args = [h] + ([delta] if has_delta else []) + [w.reshape(1, d)]
    outs = pl.pallas_call(
        functools.partial(_addnorm_kernel, lp=lp, pad=pad, has_delta=has_delta, want_h=want_h),
        grid=(m // tr,), in_specs=in_specs, out_specs=out_specs, out_shape=out_shape,
        compiler_params=_params(("parallel",), 12 * tr * d * 4),
        name="addnorm",
    )(*args)
    return outs if want_h else (None, outs[0])


def _final_norm_kernel(h_ref, d_ref, w_ref, o_ref):
    h = h_ref[0] + d_ref[0]
    o_ref[0] = h * lax.rsqrt(jnp.mean(h * h, axis=-1, keepdims=True) + EPS) * w_ref[...]


def _final_norm(h, delta, w, *, skip):
    b, lp, d = h.shape
    seq = lp - skip
    tr = _pick(seq, 512, SUBLANES)
    assert skip % SUBLANES == 0
    in_spec = pl.BlockSpec((pl.Element(1), pl.Element(tr), pl.Element(d)),
                           lambda bi, i: (bi, pl.multiple_of(skip + i * tr, SUBLANES), 0))
    return pl.pallas_call(
        _final_norm_kernel,
        grid=(b, seq // tr),
        in_specs=[in_spec, in_spec, pl.BlockSpec((1, d), lambda bi, i: (0, 0))],
        out_specs=pl.BlockSpec((1, tr, d), lambda bi, i: (bi, i, 0)),
        out_shape=jax.ShapeDtypeStruct((b, seq, d), F32),
        compiler_params=_params(("parallel", "parallel"), 12 * tr * d * 4),
        name="final_norm",
    )(h, delta, w.reshape(1, d))


def _mm_kernel(x_ref, w_ref, *rest):
    o_ref = rest[-1]
    y = jnp.dot(x_ref[...], w_ref[0].astype(BF16), preferred_element_type=F32)
    if len(rest) == 2:
        y = rest[0][...] + y
    o_ref[...] = y.astype(o_ref.dtype)


def _mm(x, w, layer, *, n_out=None, res=None, tn=512, out_dtype=F32, name="mm"):
    m, k = x.shape
    n_out = w.shape[2] if n_out is None else n_out
    tn = min(tn, n_out)
    assert n_out % tn == 0
    tm = _pick(m, 2080, 16)
    vmem = 2 * (tm * k * 2 + k * tn * 4 + 2 * tm * tn * 4) + k * tn * 2 + tm * tn * 4
    out_spec = pl.BlockSpec((tm, tn), lambda i, j: (i, j))
    return pl.pallas_call(
        _mm_kernel,
        grid=(m // tm, n_out // tn),
        in_specs=[pl.BlockSpec((tm, k), lambda i, j: (i, 0)),
                  pl.BlockSpec((1, k, tn), lambda i, j: (layer, 0, j))] + ([] if res is None else [out_spec]),
        out_specs=out_spec,
        out_shape=jax.ShapeDtypeStruct((m, n_out), out_dtype),
        compiler_params=_params(("parallel", "arbitrary"), vmem + (4 << 20)),
        name=name,
    )(*((x, w) if res is None else (x, w, res)))


def _mm_nt_kernel(x_ref, wt_ref, o_ref):
    y = lax.dot_general(x_ref[...], wt_ref[0].astype(BF16), (((1,), (1,)), ((), ())), preferred_element_type=F32)
    o_ref[...] = y.astype(o_ref.dtype)


def _mm_nt(x, wt, layer, row0, n_out, *, tn=512, out_dtype=F32, name="mm_nt"):
    m, k = x.shape
    tn = min(tn, n_out)
    assert n_out % tn == 0 and row0 % SUBLANES == 0 and row0 + n_out <= wt.shape[1]
    tm = _pick(m, 2080, 16)
    vmem = 2 * (tm * k * 2 + k * tn * 4 + tm * tn * 4) + 2 * k * tn * 2 + tm * tn * 4
    return pl.pallas_call(
        _mm_nt_kernel,
        grid=(m // tm, n_out // tn),
        in_specs=[pl.BlockSpec((tm, k), lambda i, j: (i, 0)),
                  pl.BlockSpec((pl.Element(1), pl.Element(tn), pl.Element(k)), lambda i, j: (layer, pl.multiple_of(row0 + j * tn, SUBLANES), 0))],
        out_specs=pl.BlockSpec((tm, tn), lambda i, j: (i, j)),
        out_shape=jax.ShapeDtypeStruct((m, n_out), out_dtype),
        compiler_params=_params(("parallel", "arbitrary"), vmem + (4 << 20)),
        name=name,
    )(x, wt)


GDN_ROWS_PER_STEP = 1


def _gdn_pipe_kernel(q_ref, k_ref, v_ref, qh_ref, kh_ref, vh_ref, ba_ref, cw_ref, gp_ref, z_ref, nw_ref,
                     o_ref, s_ref, ext_ref, pu_ref, pw_ref, pqd_ref, pkd_ref, pqk_ref, pcd_ref, *, heads, pad, nc):
    c = CHUNK
    hd = heads * DN_DK
    bsz = q_ref.shape[0]
    n = pl.program_id(1)
    slot = n % 2
    chunk = jnp.minimum(n, nc - 1)

    @pl.when(n == 0)
    def _():
        for ref in (s_ref, pu_ref, pw_ref, pqd_ref, pkd_ref, pqk_ref, pcd_ref):
            ref[...] = jnp.zeros_like(ref)

    for b in range(bsz):
        prev = (1 - slot) * bsz + b
        cd = pcd_ref[prev][0:1, :]
        for h in range(heads):
            sl = slice(h * DN_DV, (h + 1) * DN_DV)
            s = s_ref[b * heads + h]
            ws_qs = jnp.dot(jnp.concatenate([pw_ref[prev, :, sl], pqd_ref[prev, :, sl]], axis=0).astype(BF16),
                            s.astype(BF16), preferred_element_type=F32)
            v_new = (pu_ref[prev, :, sl] - ws_qs[0:c]).astype(BF16)
            o = ws_qs[c:2 * c] + jnp.dot(pqk_ref[prev, :, h * 2 * c:h * 2 * c + c].astype(BF16), v_new,
                                         preferred_element_type=F32)
            s_ref[b * heads + h] = s * cd[:, h:h + 1] + lax.dot_general(
                pkd_ref[prev, :, sl].astype(BF16), v_new, (((0,), (0,)), ((), ())), preferred_element_type=F32)
            on = o * lax.rsqrt(jnp.mean(o * o, axis=-1, keepdims=True) + EPS) * nw_ref[...]
            o_ref[b, :, sl] = (on * _silu(z_ref[b, :, sl])).astype(o_ref.dtype)

    def conv_silu(main_ref, halo_ref, b, part):
        e = b * 3 + part
        ext_ref[e, 0:8, :] = halo_ref[b]
        ext_ref[e, 8:8 + c, :] = main_ref[b]
        kw = cw_ref.shape[0]
        acc = jnp.zeros((c, hd), F32)
        for j in range(kw):
            acc = acc + cw_ref[j:j + 1, part * hd:(part + 1) * hd] * ext_ref[e, 8 - (kw - 1) + j:8 - (kw - 1) + j + c, :]
        return _silu(acc)

    def bdot(a, b):
        return jnp.dot(a.astype(BF16), b.astype(BF16), preferred_element_type=F32)

    row = chunk * c + lax.broadcasted_iota(jnp.int32, (c, 1), 0)
    live = row >= pad
    ii = lax.broadcasted_iota(jnp.int32, (2 * c, c), 0)
    jj = lax.broadcasted_iota(jnp.int32, (2 * c, c), 1)
    tri_pad = jnp.where((ii >= jj) & (ii < c), 1.0, 0.0).astype(F32)
    i2 = lax.broadcasted_iota(jnp.int32, (c, 2 * c), 0)
    j2 = lax.broadcasted_iota(jnp.int32, (c, 2 * c), 1)
    causal = i2 >= j2
    i1 = lax.broadcasted_iota(jnp.int32, (c, c), 0)
    j1 = lax.broadcasted_iota(jnp.int32, (c, c), 1)
    strict = i1 > j1

    for b in range(bsz):
        cur = slot * bsz + b
        ba = ba_ref[b]
        alpha = pltpu.roll(ba, LANES - heads, axis=1)
        beta = jnp.where(live, _sigmoid(ba), 0.0)
        g = jnp.where(live, -jnp.exp(gp_ref[0:1, :]) * _softplus(alpha + gp_ref[1:2, :]), 0.0)
        gc_pad = jnp.dot(tri_pad, g, precision=HI, preferred_element_type=F32)
        gc = gc_pad[0:c]
        gct = gc_pad.T
        eg = jnp.exp(gc)
        erev = jnp.exp(gc[c - 1:c, :] - gc)
        pcd_ref[cur] = jnp.broadcast_to(jnp.exp(gc[c - 1:c, :]), (SUBLANES, LANES))

        qa = conv_silu(q_ref, qh_ref, b, 0)
        ka = conv_silu(k_ref, kh_ref, b, 1)
        va = conv_silu(v_ref, vh_ref, b, 2)

        a_list, rhs_list = [], []
        for h in range(heads):
            sl = slice(h * DN_DK, (h + 1) * DN_DK)
            qh = qa[:, sl]
            kh = ka[:, sl]
            qh = qh * lax.rsqrt(jnp.sum(qh * qh, axis=-1, keepdims=True) + EPS) * (DN_DK ** -0.5)
            kh = kh * lax.rsqrt(jnp.sum(kh * kh, axis=-1, keepdims=True) + EPS)
            bh = beta[:, h:h + 1]
            egh = eg[:, h:h + 1]
            kb = kh * bh
            decay = jnp.exp(jnp.where(causal, gc[:, h:h + 1] - gct[h:h + 1, :], NEG_BIG))
            k_pad = jnp.concatenate([kh, jnp.zeros_like(kh)], axis=0)
            lhs = jnp.concatenate([kb, qh], axis=0).astype(BF16)
            kq = lax.dot_general(lhs, k_pad.astype(BF16), (((1,), (1,)), ((), ())), preferred_element_type=F32)
            a_list.append(jnp.where(strict, kq[0:c, 0:c] * decay[:, 0:c], 0.0))
            pqk_ref[cur, :, h * 2 * c:(h + 1) * 2 * c] = kq[c:2 * c, :] * decay
            rhs_list.append(jnp.concatenate([va[:, sl] * bh, kb * egh], axis=1))
            pqd_ref[cur, :, sl] = qh * egh
            pkd_ref[cur, :, sl] = kh * erev[:, h:h + 1]

        m_list = [-a for a in a_list]
        p_list = a_list
        span = 2
        while span < c:
            p_list = [bdot(p, p) for p in p_list]
            m_list = [mm + p + bdot(mm, p) for mm, p in zip(m_list, p_list)]
            span *= 2
        for h in range(heads):
            sl = slice(h * DN_DK, (h + 1) * DN_DK)
            sol = rhs_list[h] + bdot(m_list[h], rhs_list[h])
            pu_ref[cur, :, sl] = sol[:, 0:DN_DV]
            pw_ref[cur, :, sl] = sol[:, DN_DV:DN_DV + DN_DK]


def _gdn_pipe(qkvz, ba, conv_w, gate_params, norm_w, *, heads, pad):
    b, lp, _ = qkvz.shape
    hd = heads * DN_DK
    nc = lp // CHUNK
    rb = GDN_ROWS_PER_STEP
    assert b % rb == 0
    cur = lambda n: jnp.minimum(n, nc - 1)
    main = lambda part: pl.BlockSpec((rb, CHUNK, hd), lambda bi, n: (bi, cur(n), part))
    halo = lambda part: pl.BlockSpec((rb, 8, hd), lambda bi, n: (bi, jnp.maximum(cur(n) * (CHUNK // 8) - 1, 0), part))
    lag = lambda width, col: pl.BlockSpec((rb, CHUNK, width), lambda bi, n: (bi, jnp.maximum(n - 1, 0), col))
    slots = lambda width: pltpu.VMEM((2 * rb, CHUNK, width), F32)
    return pl.pallas_call(
        functools.partial(_gdn_pipe_kernel, heads=heads, pad=pad, nc=nc),
        grid=(b // rb, nc + 1),
        in_specs=[main(0), main(1), main(2), halo(0), halo(1), halo(2),
                  pl.BlockSpec((rb, CHUNK, LANES), lambda bi, n: (bi, cur(n), 0)),
                  pl.BlockSpec(conv_w.shape, lambda bi, n: (0, 0)),
                  pl.BlockSpec(gate_params.shape, lambda bi, n: (0, 0)),
                  lag(hd, 3), pl.BlockSpec((1, DN_DV), lambda bi, n: (0, 0))],
        out_specs=lag(hd, 0),
        out_shape=jax.ShapeDtypeStruct((b, lp, hd), BF16),
        scratch_shapes=[pltpu.VMEM((rb * heads, DN_DK, DN_DV), F32), pltpu.VMEM((3 * rb, 8 + CHUNK, hd), F32),
                        slots(hd), slots(hd), slots(hd), slots(hd), slots(heads * 2 * CHUNK),
                        pltpu.VMEM((2 * rb, SUBLANES, LANES), F32)],
        compiler_params=_params(("parallel", "arbitrary"), 48 * rb * CHUNK * hd * 4),
        name="gdn_pipe",
    )(qkvz, qkvz, qkvz, qkvz, qkvz, qkvz, ba, conv_w, gate_params, qkvz, norm_w.reshape(1, DN_DV))


CONV_HALO = 32
CONV_CT = LANES


def _conformer_kernel(a_ref, g_ref, ah_ref, gh_ref, cw_ref, cb_ref, lg_ref, lb_ref, o_ref, ext_ref, sh_ref, y_ref):
    r, ch = o_ref.shape[1], o_ref.shape[2]
    kw = cw_ref.shape[0]
    ext_ref[0:CONV_HALO, :] = ah_ref[0].astype(F32) * _sigmoid(gh_ref[0].astype(F32))
    ext_ref[CONV_HALO:CONV_HALO + r, :] = a_ref[0].astype(F32) * _sigmoid(g_ref[0].astype(F32))
    base = CONV_HALO - (kw - 1)
    span = sh_ref.shape[1]

    def ch_body(ci, carry):
        c0 = pl.multiple_of(ci * CONV_CT, CONV_CT)
        for s in range(1, SUBLANES):
            sh_ref[s - 1] = ext_ref[s:s + span, pl.ds(c0, CONV_CT)]
        acc = jnp.broadcast_to(cb_ref[:, pl.ds(c0, CONV_CT)], (r, CONV_CT))
        for j in range(kw):
            q, s = divmod(base + j, SUBLANES)
            rows = slice(q * SUBLANES, q * SUBLANES + r)
            tap = sh_ref[s - 1, rows, :] if s else ext_ref[rows, pl.ds(c0, CONV_CT)]
            acc = acc + cw_ref[j:j + 1, pl.ds(c0, CONV_CT)] * tap
        y_ref[:, pl.ds(c0, CONV_CT)] = acc
        return carry

    lax.fori_loop(0, ch // CONV_CT, ch_body, 0)
    y = y_ref[...]
    mu = jnp.mean(y, axis=-1, keepdims=True)
    yc = y - mu
    var = jnp.mean(yc * yc, axis=-1, keepdims=True)
    yn = yc * lax.rsqrt(var + EPS) * lg_ref[...] + lb_ref[...]
    o_ref[0] = _silu(yn).astype(o_ref.dtype)


def _conformer(tail, conv_w, conv_b, ln_g, ln_b):
    b, lp, _ = tail.shape
    ch = conv_w.shape[1]
    assert conv_w.shape[0] - 1 <= CONV_HALO and ch % CONV_CT == 0
    r = _pick(lp, 320, CONV_HALO)
    main = lambda col: pl.BlockSpec((1, r, ch), lambda bi, i: (bi, i, col))
    halo = lambda col: pl.BlockSpec((1, CONV_HALO, ch), lambda bi, i: (bi, jnp.maximum(i * (r // CONV_HALO) - 1, 0), col))
    vec = pl.BlockSpec((1, ch), lambda bi, i: (0, 0))
    return pl.pallas_call(
        _conformer_kernel,
        grid=(b, lp // r),
        in_specs=[main(0), main(1), halo(0), halo(1), pl.BlockSpec(conv_w.shape, lambda bi, i: (0, 0)), vec, vec, vec],
        out_specs=pl.BlockSpec((1, r, ch), lambda bi, i: (bi, i, 0)),
        out_shape=jax.ShapeDtypeStruct((b, lp, ch), BF16),
        scratch_shapes=[pltpu.VMEM((CONV_HALO + r, ch), F32),
                        pltpu.VMEM((SUBLANES - 1, CONV_HALO + r - SUBLANES, CONV_CT), F32),
                        pltpu.VMEM((r, ch), F32)],
        compiler_params=_params(("parallel", "arbitrary"), 14 * r * ch * 4),
        name="conformer",
    )(tail, tail, tail, tail, conv_w, conv_b.reshape(1, ch), ln_g.reshape(1, ch), ln_b.reshape(1, ch))


def _branch_kernel(od_ref, oc_ref, wd_ref, wc_ref, gd_ref, gc_ref, o_ref):
    yd = jnp.dot(od_ref[...], wd_ref[0].astype(BF16), preferred_element_type=F32)
    yc = jnp.dot(oc_ref[...], wc_ref[0].astype(BF16), preferred_element_type=F32)
    o_ref[...] = (_sigmoid(gd_ref[...].astype(F32)) * yd + _sigmoid(gc_ref[...].astype(F32)) * yc).astype(o_ref.dtype)


def _branches(o_dn, o_cv, w_dn_out, w_conv_out, layer, tail, gate_col0):
    m, k = o_dn.shape
    d = w_dn_out.shape[2]
    assert o_cv.shape == o_dn.shape and w_conv_out.shape == w_dn_out.shape
    tn = min(512, d)
    tm = _pick(m, 1040, 16)
    g0 = gate_col0 // tn
    assert gate_col0 % tn == 0 and d % tn == 0
    x_spec = pl.BlockSpec((tm, k), lambda i, j: (i, 0))
    w_spec = pl.BlockSpec((1, k, tn), lambda i, j: (layer, 0, j))
    vmem = 2 * (2 * tm * k * 2 + 2 * k * tn * 4 + 2 * tm * tn * 4 + tm * tn * 2) + 2 * k * tn * 2 + 4 * tm * tn * 4
    return pl.pallas_call(
        _branch_kernel,
        grid=(m // tm, d // tn),
        in_specs=[x_spec, x_spec, w_spec, w_spec,
                  pl.BlockSpec((tm, tn), lambda i, j: (i, g0 + j)),
                  pl.BlockSpec((tm, tn), lambda i, j: (i, g0 + d // tn + j))],
        out_specs=pl.BlockSpec((tm, tn), lambda i, j: (i, j)),
        out_shape=jax.ShapeDtypeStruct((m, d), BF16),
        compiler_params=_params(("parallel", "arbitrary"), vmem + (4 << 20)),
        name="branches",
    )(o_dn, o_cv, w_dn_out, w_conv_out, tail, tail)


def _ffn_kernel(x_ref, w1_ref, w3_ref, w2_ref, o_ref):
    @pl.when(pl.program_id(1) == 0)
    def _():
        o_ref[...] = jnp.zeros_like(o_ref)

    x = x_ref[...]
    h1 = jnp.dot(x, w1_ref[0].astype(BF16), preferred_element_type=F32)
    h3 = jnp.dot(x, w3_ref[0].astype(BF16), preferred_element_type=F32)
    act = (_silu(h1) * h3).astype(BF16)
    o_ref[...] += jnp.dot(act, w2_ref[0].astype(BF16), preferred_element_type=F32)


def _ffn(x, w1, w3, w2, layer):
    m, d = x.shape
    ff = w1.shape[2]
    tm = _pick(m, 1040, 16)
    tf = _pick(ff, 256, LANES)
    vmem = 2 * (tm * d * 2 + 3 * d * tf * 4 + tm * d * 4) + 3 * d * tf * 2 + 4 * tm * tf * 4 + tm * d * 4
    return pl.pallas_call(
        _ffn_kernel,
        grid=(m // tm, ff // tf),
        in_specs=[pl.BlockSpec((tm, d), lambda i, f: (i, 0)),
                  pl.BlockSpec((1, d, tf), lambda i, f: (layer, 0, f)),
                  pl.BlockSpec((1, d, tf), lambda i, f: (layer, 0, f)),
                  pl.BlockSpec((1, tf, d), lambda i, f: (layer, f, 0))],
        out_specs=pl.BlockSpec((tm, d), lambda i, f: (i, 0)),
        out_shape=jax.ShapeDtypeStruct((m, d), F32),
        compiler_params=_params(("parallel", "arbitrary"), vmem + (4 << 20)),
        name="ffn",
    )(x, w1, w3, w2)


def _router_kernel(h_ref, nw_ref, rw_ref, u_ref, wt_ref, ix_ref, cnt_ref, *, n_exp):
    @pl.when(pl.program_id(0) == 0)
    def _():
        cnt_ref[...] = jnp.zeros_like(cnt_ref)

    h = h_ref[...]
    tr = h.shape[0]
    u = h * lax.rsqrt(jnp.mean(h * h, axis=-1, keepdims=True) + EPS) * nw_ref[...]
    u_ref[...] = u.astype(u_ref.dtype)
    logits = jnp.dot(u, rw_ref[...], precision=HI, preferred_element_type=F32)
    lane = lax.broadcasted_iota(jnp.int32, logits.shape, 1)
    logits = jnp.where(lane < n_exp, logits, NEG_BIG)
    m1 = jnp.max(logits, axis=-1, keepdims=True)
    i1 = jnp.min(jnp.where(logits == m1, lane, LANES), axis=-1, keepdims=True)
    rest = jnp.where(lane == i1, NEG_BIG, logits)
    m2 = jnp.max(rest, axis=-1, keepdims=True)
    i2 = jnp.min(jnp.where(rest == m2, lane, LANES), axis=-1, keepdims=True)
    e2 = jnp.exp(m2 - m1)
    den = 1.0 + e2
    wt_ref[...] = jnp.where(lane == 0, 1.0 / den, jnp.where(lane == 1, e2 / den, 0.0))

    chosen = jnp.where((lane == i1) | (lane == i2), 1.0, 0.0)
    ri = lax.broadcasted_iota(jnp.int32, (tr, tr), 0)
    ci = lax.broadcasted_iota(jnp.int32, (tr, tr), 1)
    before = jnp.where(ri > ci, 1.0, 0.0).astype(BF16)
    prefix = jnp.dot(before, chosen.astype(BF16), preferred_element_type=F32) + cnt_ref[0:1, :]
    ra = jnp.sum(jnp.where(lane == i1, prefix, 0.0), axis=-1, keepdims=True).astype(jnp.int32)
    rb = jnp.sum(jnp.where(lane == i2, prefix, 0.0), axis=-1, keepdims=True).astype(jnp.int32)
    ix_ref[...] = jnp.where(lane == 0, i1, jnp.where(lane == 1, i2, jnp.where(lane == 2, ra, jnp.where(lane == 3, rb, 0))))
    cnt_ref[...] = cnt_ref[...] + jnp.sum(chosen, axis=0, keepdims=True)


def _router(h, norm_w, router_w):
    m, d = h.shape
    n_exp = router_w.shape[1]
    assert 2 <= n_exp <= LANES
    rw = jnp.pad(router_w, ((0, 0), (0, LANES - n_exp)))
    tr = _pick(m, 520, 8)
    row_spec = pl.BlockSpec((tr, d), lambda i: (i, 0))
    lane_spec = pl.BlockSpec((tr, LANES), lambda i: (i, 0))
    return pl.pallas_call(
        functools.partial(_router_kernel, n_exp=n_exp),
        grid=(m // tr,),
        in_specs=[row_spec, pl.BlockSpec((1, d), lambda i: (0, 0)), pl.BlockSpec((d, LANES), lambda i: (0, 0))],
        out_specs=[row_spec, lane_spec, lane_spec, pl.BlockSpec((8, LANES), lambda i: (0, 0))],
        out_shape=[jax.ShapeDtypeStruct((m, d), BF16), jax.ShapeDtypeStruct((m, LANES), F32),
                   jax.ShapeDtypeStruct((m, LANES), jnp.int32), jax.ShapeDtypeStruct((8, LANES), F32)],
        compiler_params=_params(("arbitrary",), 12 * tr * d * 4),
        name="router",
    )(h, norm_w.reshape(1, d), rw)


MOE_TILE = 1152
MOE_SUBS = 3


def _moe_tile_rows(m):
    return MOE_TILE if m >= 4 * MOE_TILE else 96


def _dispatch_kernel(pa_ref, pb_ref, u_ref, z_ref, xs_ref, sem):
    del z_ref
    i = pl.program_id(0)
    tr = u_ref.shape[0]

    def copy(r, pos_ref):
        return pltpu.make_async_copy(u_ref.at[r], xs_ref.at[pos_ref[i * tr + r]], sem)

    def issue(r, carry):
        copy(r, pa_ref).start()
        copy(r, pb_ref).start()
        return carry

    lax.fori_loop(0, tr, issue, 0)
    for _ in range(2):
        pltpu.make_async_copy(u_ref, xs_ref.at[pl.ds(0, tr)], sem).wait()


def _dispatch(u3, pos_a, pos_b, n_rows):
    m, s, l = u3.shape
    tr = _pick(m, 520, 8)
    return pl.pallas_call(
        _dispatch_kernel,
        grid_spec=pltpu.PrefetchScalarGridSpec(
            num_scalar_prefetch=2, grid=(m // tr,),
            in_specs=[pl.BlockSpec((tr, s, l), lambda i, pa, pb: (i, 0, 0)), pl.BlockSpec(memory_space=pl.ANY)],
            out_specs=pl.BlockSpec(memory_space=pl.ANY),
            scratch_shapes=[pltpu.SemaphoreType.DMA]),
        out_shape=jax.ShapeDtypeStruct((n_rows, s, l), u3.dtype),
        input_output_aliases={3: 0},
        compiler_params=_params(("arbitrary",), 4 * tr * s * l * 2),
        name="moe_dispatch",
    )(pos_a, pos_b, u3, jnp.zeros((n_rows, s, l), u3.dtype))


def _moe_ffn_kernel(te_ref, rows_ref, src_ref, x_ref, w1_ref, w3_ref, w2_ref, o_ref, w1b_ref, w3b_ref, w2b_ref):
    del te_ref, src_ref
    s = pl.program_id(0)
    f = pl.program_id(1)
    rows = rows_ref[s]
    sub = x_ref.shape[0] // MOE_SUBS

    def block(i):
        sl = slice(i * sub, (i + 1) * sub)
        x = x_ref[sl, :]
        h1 = jnp.dot(x, w1b_ref[...], preferred_element_type=F32)
        h3 = jnp.dot(x, w3b_ref[...], preferred_element_type=F32)
        act = (_silu(h1) * h3).astype(BF16)
        o_ref[sl, :] += jnp.dot(act, w2b_ref[...], preferred_element_type=F32)

    @pl.when(f == 0)
    def _():
        o_ref[...] = jnp.zeros_like(o_ref)

    full = rows > (MOE_SUBS - 1) * sub

    @pl.when(full)
    def _():
        x = x_ref[...]
        h1 = jnp.dot(x, w1_ref[0].astype(BF16), preferred_element_type=F32)
        h3 = jnp.dot(x, w3_ref[0].astype(BF16), preferred_element_type=F32)
        act = (_silu(h1) * h3).astype(BF16)
        o_ref[...] += jnp.dot(act, w2_ref[0].astype(BF16), preferred_element_type=F32)

    @pl.when(jnp.logical_and(rows > 0, jnp.logical_not(full)))
    def _():
        w1b_ref[...] = w1_ref[0].astype(BF16)
        w3b_ref[...] = w3_ref[0].astype(BF16)
        w2b_ref[...] = w2_ref[0].astype(BF16)
        block(0)

    for i in range(1, MOE_SUBS - 1):
        @pl.when(jnp.logical_and(rows > i * sub, jnp.logical_not(full)))
        def _():
            block(i)


def _moe_ffn(xs, w1, w3, w2, tile_expert, tile_rows, tile_src, r):
    p, d = xs.shape
    ff = w1.shape[2]
    tf = _pick(ff, 256, LANES)
    nf = ff // tf
    f_of = lambda s, f, rows: jnp.where(rows[s] > 0, f, nf - 1)
    vmem = 2 * (r * d * 2 + 3 * d * tf * 4 + r * d * 4) + 3 * d * tf * 2 + 6 * (r // MOE_SUBS) * tf * 4
    return pl.pallas_call(
        _moe_ffn_kernel,
        grid_spec=pltpu.PrefetchScalarGridSpec(
            num_scalar_prefetch=3, grid=(p // r, nf),
            in_specs=[pl.BlockSpec((r, d), lambda s, f, te, rows, src: (src[s], 0)),
                      pl.BlockSpec((1, d, tf), lambda s, f, te, rows, src: (te[s], 0, f_of(s, f, rows))),
                      pl.BlockSpec((1, d, tf), lambda s, f, te, rows, src: (te[s], 0, f_of(s, f, rows))),
                      pl.BlockSpec((1, tf, d), lambda s, f, te, rows, src: (te[s], f_of(s, f, rows), 0))],
            out_specs=pl.BlockSpec((r, d), lambda s, f, te, rows, src: (s, 0)),
            scratch_shapes=[pltpu.VMEM((d, tf), BF16), pltpu.VMEM((d, tf), BF16), pltpu.VMEM((tf, d), BF16)]),
        out_shape=jax.ShapeDtypeStruct((p, d), F32),
        compiler_params=_params(("arbitrary", "arbitrary"), vmem + (4 << 20)),
        name="moe_ffn",
    )(tile_expert, tile_rows, tile_src, xs, w1, w3, w2)


def _combine_kernel(pa_ref, pb_ref, wt_ref, ys_ref, o_ref, bufa_ref, bufb_ref, sem):
    i = pl.program_id(0)
    tr = o_ref.shape[0]

    def copy(r, pos_ref, buf_ref):
        return pltpu.make_async_copy(ys_ref.at[pl.ds(pos_ref[i * tr + r], 1), :], buf_ref.at[pl.ds(r, 1), :], sem)

    def issue(r, carry):
        copy(r, pa_ref, bufa_ref).start()
        copy(r, pb_ref, bufb_ref).start()
        return carry

    lax.fori_loop(0, tr, issue, 0)
    pltpu.make_async_copy(ys_ref.at[pl.ds(0, tr), :], bufa_ref, sem).wait()
    pltpu.make_async_copy(ys_ref.at[pl.ds(0, tr), :], bufb_ref, sem).wait()
    o_ref[...] = wt_ref[:, 0:1] * bufa_ref[...] + wt_ref[:, 1:2] * bufb_ref[...]


def _combine(ys, wt, pos_a, pos_b):
    m = wt.shape[0]
    d = ys.shape[1]
    tr = _pick(m, 520, 8)
    return pl.pallas_call(
        _combine_kernel,
        grid_spec=pltpu.PrefetchScalarGridSpec(
            num_scalar_prefetch=2, grid=(m // tr,),
            in_specs=[pl.BlockSpec((tr, LANES), lambda i, pa, pb: (i, 0)), pl.BlockSpec(memory_space=pl.ANY)],
            out_specs=pl.BlockSpec((tr, d), lambda i, pa, pb: (i, 0)),
            scratch_shapes=[pltpu.VMEM((tr, d), F32), pltpu.VMEM((tr, d), F32), pltpu.SemaphoreType.DMA]),
        out_shape=jax.ShapeDtypeStruct((m, d), F32),
        compiler_params=_params(("arbitrary",), 8 * tr * d * 4),
        name="moe_combine",
    )(pos_a, pos_b, wt, ys)


def _moe(h, norm_w, router_w, w1, w3, w2, e0):
    m, d = h.shape
    n_exp = router_w.shape[1]
    r = _moe_tile_rows(m)
    n_tiles = (2 * m + n_exp * (r - 1) + r - 1) // r
    u, wt, ix, cnt = _router(h, norm_w, router_w)

    cnt = cnt[0, :n_exp].astype(jnp.int32)
    nt = (cnt + r - 1) // r
    tend = jnp.cumsum(nt)
    tstart = tend - nt
    pos_a = tstart[ix[:, 0]] * r + ix[:, 2]
    pos_b = tstart[ix[:, 1]] * r + ix[:, 3]
    n_used = tend[n_exp - 1]
    s = jnp.arange(n_tiles, dtype=jnp.int32)
    tile_src = jnp.clip(s, 0, jnp.maximum(n_used - 1, 0))
    tile_expert = jnp.minimum(jnp.searchsorted(tend, tile_src, side="right"), n_exp - 1).astype(jnp.int32)
    tile_rows = jnp.where(s < n_used, jnp.clip(cnt[tile_expert] - (s - tstart[tile_expert]) * r, 0, r), 0).astype(jnp.int32)

    assert d % LANES == 0
    xs = _dispatch(u.reshape(m, d // LANES, LANES), pos_a, pos_b, n_tiles * r)
    ys = _moe_ffn(xs.reshape(n_tiles * r, d), w1, w3, w2, tile_expert + e0, tile_rows, tile_src, r)
    return _combine(ys, wt, pos_a, pos_b)


def kernel(x, meta_tokens, attn_norm, w_in, short_conv_w, a_log, dt_bias, dn_norm, w_dn_out, dw_conv_w, dw_conv_b,
           conv_ln_g, conv_ln_b, w_conv_out, w_merge_out, ffn_norm, dense_w1, dense_w3, dense_w2, router_w,
           moe_w1, moe_w3, moe_w2, final_norm):
    bsz, seq, d = x.shape
    n_meta = meta_tokens.shape[0]
    depth, heads = a_log.shape
    pad = (-n_meta) % CHUNK
    lp = pad + n_meta + seq
    m = bsz * lp
    hd = heads * DN_DK
    ch = dw_conv_w.shape[2]
    assert lp % CHUNK == 0 and DN_DK == DN_DV == LANES and heads <= LANES
    off_b = 3 * hd + heads * DN_DV
    off_glu = off_b + 2 * heads
    assert w_in.shape[2] == off_glu + 2 * ch + 2 * d

    w_t = jnp.swapaxes(w_in, 1, 2)
    assert off_b + LANES <= w_in.shape[2] and 2 * heads <= LANES

    meta = jnp.broadcast_to(meta_tokens[None].astype(x.dtype), (bsz, n_meta, d))
    h = jnp.concatenate([jnp.zeros((bsz, pad, d), x.dtype), meta, x], axis=1).reshape(m, d)

    delta = None
    for i in range(depth):
        h_new, u = _addnorm(h, delta, attn_norm[i], lp=lp, pad=pad, want_h=delta is not None)
        h = h if delta is None else h_new
        qkvz = _mm_nt(u, w_t, i, 0, off_b, name="in_proj_qkvz")
        ba = _mm_nt(u, w_t, i, off_b, LANES, name="in_proj_ba")
        tail = _mm_nt(u, w_t, i, off_glu, 2 * ch + 2 * d, out_dtype=BF16,
                      name="in_proj_tail")
        gate_params = jnp.zeros((8, LANES), F32)
        gate_params = gate_params.at[0, 0:heads].set(a_log[i]).at[1, 0:heads].set(dt_bias[i])

        qkvz3 = qkvz.reshape(bsz, lp, off_b)
        tail3 = tail.reshape(bsz, lp, 2 * ch + 2 * d)
        o_dn = _gdn_pipe(qkvz3, ba.reshape(bsz, lp, LANES), short_conv_w[i], gate_params, dn_norm[i], heads=heads, pad=pad)
        o_cv = _conformer(tail3, dw_conv_w[i], dw_conv_b[i], conv_ln_g[i], conv_ln_b[i])
        merged = _branches(o_dn.reshape(m, hd), o_cv.reshape(m, ch), w_dn_out, w_conv_out, i, tail, 2 * ch)
        h = _mm(merged, w_merge_out, i, res=h, name="merge_out")

        j = i // 2
        if i % 2 == 0:
            _, u = _addnorm(h, None, ffn_norm[i], lp=lp, pad=0, want_h=False)
            delta = _ffn(u, dense_w1, dense_w3, dense_w2, j)
        else:
            n_exp = moe_w1.shape[1]
            flat = lambda t: t.reshape((-1,) + t.shape[2:])
            delta = _moe(h, ffn_norm[i], router_w[j], flat(moe_w1), flat(moe_w3), flat(moe_w2), j * n_exp)

    return _final_norm(h.reshape(bsz, lp, d), delta.reshape(bsz, lp, d), final_norm, skip=pad + n_meta)
```

```python
import functools

import jax
import jax.numpy as jnp
from jax import lax
from jax.experimental import pallas as pl
from jax.experimental.pallas import tpu as pltpu

DN_DK = 128
DN_DV = 128
CHUNK = 64
EPS = 1e-6
LANES = 128
SUBLANES = 8
NEG_BIG = -1e30
V7X_VMEM_BYTES = 64 * 1024 * 1024
VMEM_CAP = V7X_VMEM_BYTES - 8 * 1024 * 1024

F32 = jnp.float32
BF16 = jnp.bfloat16
HI = lax.Precision.HIGHEST


def _pick(n, target, mult):
    best = None
    for d in range(mult, min(n, target) + 1, mult):
        if n % d == 0:
            best = d
    assert best is not None, (n, target, mult)
    return best


def _params(sem, vmem_bytes):
    return pltpu.CompilerParams(dimension_semantics=sem,
                                vmem_limit_bytes=int(min(VMEM_CAP, max(vmem_bytes, 16 * 1024 * 1024))))


def _sigmoid(x):
    return 0.5 * jnp.tanh(0.5 * x) + 0.5


def _silu(x):
    return x * _sigmoid(x)


def _softplus(x):
    return jnp.maximum(x, 0.0) + jnp.log(1.0 + jnp.exp(-jnp.abs(x)))


def _addnorm_kernel(*refs, lp, pad, has_delta, want_h):
    it = iter(refs)
    h_ref = next(it)
    d_ref = next(it) if has_delta else None
    w_ref = next(it)
    hn_ref = next(it) if want_h else None
    u_ref = next(it)
    tr = h_ref.shape[0]
    h = h_ref[...]
    if has_delta:
        h = h + d_ref[...]
    if want_h:
        hn_ref[...] = h
    y = h * lax.rsqrt(jnp.mean(h * h, axis=-1, keepdims=True) + EPS) * w_ref[...]
    row = pl.program_id(0) * tr + lax.broadcasted_iota(jnp.int32, (tr, 1), 0)
    y = jnp.where(row % lp >= pad, y, 0.0)
    u_ref[...] = y.astype(u_ref.dtype)


def _addnorm(h, delta, w, *, lp, pad, want_h):
    m, d = h.shape
    tr = _pick(m, 520, 8)
    has_delta = delta is not None
    row_spec = pl.BlockSpec((tr, d), lambda i: (i, 0))
    in_specs = [row_spec] + ([row_spec] if has_delta else []) + [pl.BlockSpec((1, d), lambda i: (0, 0))]
    out_shape = ([jax.ShapeDtypeStruct((m, d), F32)] if want_h else []) + [jax.ShapeDtypeStruct((m, d), BF16)]
    out_specs = ([row_spec] if want_h else []) + [row_spec]
    args = [h] + ([delta] if has_delta else []) + [w.reshape(1, d)]
    outs = pl.pallas_call(
        functools.partial(_addnorm_kernel, lp=lp, pad=pad, has_delta=has_delta, want_h=want_h),
        grid=(m // tr,), in_specs=in_specs, out_specs=out_specs, out_shape=out_shape,
        compiler_params=_params(("parallel",), 12 * tr * d * 4),
        name="addnorm",
    )(*args)
    return outs if want_h else (None, outs[0])


def _final_norm_kernel(h_ref, d_ref, w_ref, o_ref):
    h = h_ref[0] + d_ref[0]
    o_ref[0] = h * lax.rsqrt(jnp.mean(h * h, axis=-1, keepdims=True) + EPS) * w_ref[...]


def _final_norm(h, delta, w, *, skip):
    b, lp, d = h.shape
    seq = lp - skip
    tr = _pick(seq, 512, SUBLANES)
    assert skip % SUBLANES == 0
    in_spec = pl.BlockSpec((pl.Element(1), pl.Element(tr), pl.Element(d)),
                           lambda bi, i: (bi, pl.multiple_of(skip + i * tr, SUBLANES), 0))
    return pl.pallas_call(
        _final_norm_kernel,
        grid=(b, seq // tr),
        in_specs=[in_spec, in_spec, pl.BlockSpec((1, d), lambda bi, i: (0, 0))],
        out_specs=pl.BlockSpec((1, tr, d), lambda bi, i: (bi, i, 0)),
        out_shape=jax.ShapeDtypeStruct((b, seq, d), F32),
        compiler_params=_params(("parallel", "parallel"), 12 * tr * d * 4),
        name="final_norm",
    )(h, delta, w.reshape(1, d))


def _mm_kernel(x_ref, w_ref, *rest):
    o_ref = rest[-1]
    y = jnp.dot(x_ref[...], w_ref[0].astype(BF16), preferred_element_type=F32)
    if len(rest) == 2:
        y = rest[0][...] + y
    o_ref[...] = y.astype(o_ref.dtype)


def _mm(x, w, layer, *, n_out=None, res=None, tn=512, out_dtype=F32, name="mm"):
    m, k = x.shape
    n_out = w.shape[2] if n_out is None else n_out
    tn = min(tn, n_out)
    assert n_out % tn == 0
    tm = _pick(m, 2080, 16)
    vmem = 2 * (tm * k * 2 + k * tn * 4 + 2 * tm * tn * 4) + k * tn * 2 + tm * tn * 4
    out_spec = pl.BlockSpec((tm, tn), lambda i, j: (i, j))
    return pl.pallas_call(
        _mm_kernel,
        grid=(m // tm, n_out // tn),
        in_specs=[pl.BlockSpec((tm, k), lambda i, j: (i, 0)),
                  pl.BlockSpec((1, k, tn), lambda i, j: (layer, 0, j))] + ([] if res is None else [out_spec]),
        out_specs=out_spec,
        out_shape=jax.ShapeDtypeStruct((m, n_out), out_dtype),
        compiler_params=_params(("parallel", "arbitrary"), vmem + (4 << 20)),
        name=name,
    )(*((x, w) if res is None else (x, w, res)))


def _mm_nt_kernel(x_ref, wt_ref, o_ref):
    y = lax.dot_general(x_ref[...], wt_ref[0].astype(BF16), (((1,), (1,)), ((), ())), preferred_element_type=F32)
    o_ref[...] = y.astype(o_ref.dtype)


def _mm_nt(x, wt, layer, row0, n_out, *, tn=512, out_dtype=F32, name="mm_nt"):
    m, k = x.shape
    tn = min(tn, n_out)
    assert n_out % tn == 0 and row0 % SUBLANES == 0 and row0 + n_out <= wt.shape[1]
    tm = _pick(m, 2080, 16)
    vmem = 2 * (tm * k * 2 + k * tn * 4 + tm * tn * 4) + 2 * k * tn * 2 + tm * tn * 4
    return pl.pallas_call(
        _mm_nt_kernel,
        grid=(m // tm, n_out // tn),
        in_specs=[pl.BlockSpec((tm, k), lambda i, j: (i, 0)),
                  pl.BlockSpec((pl.Element(1), pl.Element(tn), pl.Element(k)), lambda i, j: (layer, pl.multiple_of(row0 + j * tn, SUBLANES), 0))],
        out_specs=pl.BlockSpec((tm, tn), lambda i, j: (i, j)),
        out_shape=jax.ShapeDtypeStruct((m, n_out), out_dtype),
        compiler_params=_params(("parallel", "arbitrary"), vmem + (4 << 20)),
        name=name,
    )(x, wt)


GDN_ROWS_PER_STEP = 1


def _gdn_pipe_kernel(q_ref, k_ref, v_ref, qh_ref, kh_ref, vh_ref, ba_ref, cw_ref, gp_ref, z_ref, nw_ref,
                     o_ref, s_ref, ext_ref, pu_ref, pw_ref, pqd_ref, pkd_ref, pqk_ref, pcd_ref, *, heads, pad, nc):
    c = CHUNK
    hd = heads * DN_DK
    bsz = q_ref.shape[0]
    n = pl.program_id(1)
    slot = n % 2
    chunk = jnp.minimum(n, nc - 1)

    @pl.when(n == 0)
    def _():
        for ref in (s_ref, pu_ref, pw_ref, pqd_ref, pkd_ref, pqk_ref, pcd_ref):
            ref[...] = jnp.zeros_like(ref)

    for b in range(bsz):
        prev = (1 - slot) * bsz + b
        cd = pcd_ref[prev][0:1, :]
        for h in range(heads):
            sl = slice(h * DN_DV, (h + 1) * DN_DV)
            s = s_ref[b * heads + h]
            ws_qs = jnp.dot(jnp.concatenate([pw_ref[prev, :, sl], pqd_ref[prev, :, sl]], axis=0).astype(BF16),
                            s.astype(BF16), preferred_element_type=F32)
            v_new = (pu_ref[prev, :, sl] - ws_qs[0:c]).astype(BF16)
            o = ws_qs[c:2 * c] + jnp.dot(pqk_ref[prev, :, h * 2 * c:h * 2 * c + c].astype(BF16), v_new,
                                         preferred_element_type=F32)
            s_ref[b * heads + h] = s * cd[:, h:h + 1] + lax.dot_general(
                pkd_ref[prev, :, sl].astype(BF16), v_new, (((0,), (0,)), ((), ())), preferred_element_type=F32)
            on = o * lax.rsqrt(jnp.mean(o * o, axis=-1, keepdims=True) + EPS) * nw_ref[...]
            o_ref[b, :, sl] = (on * _silu(z_ref[b, :, sl])).astype(o_ref.dtype)

    def conv_silu(main_ref, halo_ref, b, part):
        e = b * 3 + part
        ext_ref[e, 0:8, :] = halo_ref[b]
        ext_ref[e, 8:8 + c, :] = main_ref[b]
        kw = cw_ref.shape[0]
        acc = jnp.zeros((c, hd), F32)
        for j in range(kw):
            acc = acc + cw_ref[j:j + 1, part * hd:(part + 1) * hd] * ext_ref[e, 8 - (kw - 1) + j:8 - (kw - 1) + j + c, :]
        return _silu(acc)

    def bdot(a, b):
        return jnp.dot(a.astype(BF16), b.astype(BF16), preferred_element_type=F32)

    row = chunk * c + lax.broadcasted_iota(jnp.int32, (c, 1), 0)
    live = row >= pad
    ii = lax.broadcasted_iota(jnp.int32, (2 * c, c), 0)
    jj = lax.broadcasted_iota(jnp.int32, (2 * c, c), 1)
    tri_pad = jnp.where((ii >= jj) & (ii < c), 1.0, 0.0).astype(F32)
    i2 = lax.broadcasted_iota(jnp.int32, (c, 2 * c), 0)
    j2 = lax.broadcasted_iota(jnp.int32, (c, 2 * c), 1)
    causal = i2 >= j2
    i1 = lax.broadcasted_iota(jnp.int32, (c, c), 0)
    j1 = lax.broadcasted_iota(jnp.int32, (c, c), 1)
    strict = i1 > j1

    for b in range(bsz):
        cur = slot * bsz + b
        ba = ba_ref[b]
        alpha = pltpu.roll(ba, LANES - heads, axis=1)
        beta = jnp.where(live, _sigmoid(ba), 0.0)
        g = jnp.where(live, -jnp.exp(gp_ref[0:1, :]) * _softplus(alpha + gp_ref[1:2, :]), 0.0)
        gc_pad = jnp.dot(tri_pad, g, precision=HI, preferred_element_type=F32)
        gc = gc_pad[0:c]
        gct = gc_pad.T
        eg = jnp.exp(gc)
        erev = jnp.exp(gc[c - 1:c, :] - gc)
        pcd_ref[cur] = jnp.broadcast_to(jnp.exp(gc[c - 1:c, :]), (SUBLANES, LANES))

        qa = conv_silu(q_ref, qh_ref, b, 0)
        ka = conv_silu(k_ref, kh_ref, b, 1)
        va = conv_silu(v_ref, vh_ref, b, 2)

        a_list, rhs_list = [], []
        for h in range(heads):
            sl = slice(h * DN_DK, (h + 1) * DN_DK)
            qh = qa[:, sl]
            kh = ka[:, sl]
            qh = qh * lax.rsqrt(jnp.sum(qh * qh, axis=-1, keepdims=True) + EPS) * (DN_DK ** -0.5)
            kh = kh * lax.rsqrt(jnp.sum(kh * kh, axis=-1, keepdims=True) + EPS)
            bh = beta[:, h:h + 1]
            egh = eg[:, h:h + 1]
            kb = kh * bh
            decay = jnp.exp(jnp.where(causal, gc[:, h:h + 1] - gct[h:h + 1, :], NEG_BIG))
            k_pad = jnp.concatenate([kh, jnp.zeros_like(kh)], axis=0)
            lhs = jnp.concatenate([kb, qh], axis=0).astype(BF16)
            kq = lax.dot_general(lhs, k_pad.astype(BF16), (((1,), (1,)), ((), ())), preferred_element_type=F32)
            a_list.append(jnp.where(strict, kq[0:c, 0:c] * decay[:, 0:c], 0.0))
            pqk_ref[cur, :, h * 2 * c:(h + 1) * 2 * c] = kq[c:2 * c, :] * decay
            rhs_list.append(jnp.concatenate([va[:, sl] * bh, kb * egh], axis=1))
            pqd_ref[cur, :, sl] = qh * egh
            pkd_ref[cur, :, sl] = kh * erev[:, h:h + 1]

        m_list = [-a for a in a_list]
        p_list = a_list
        span = 2
        while span < c:
            p_list = [bdot(p, p) for p in p_list]
            m_list = [mm + p + bdot(mm, p) for mm, p in zip(m_list, p_list)]
            span *= 2
        for h in range(heads):
            sl = slice(h * DN_DK, (h + 1) * DN_DK)
            sol = rhs_list[h] + bdot(m_list[h], rhs_list[h])
            pu_ref[cur, :, sl] = sol[:, 0:DN_DV]
            pw_ref[cur, :, sl] = sol[:, DN_DV:DN_DV + DN_DK]


def _gdn_pipe(qkvz, ba, conv_w, gate_params, norm_w, *, heads, pad):
    b, lp, _ = qkvz.shape
    hd = heads * DN_DK
    nc = lp // CHUNK
    rb = GDN_ROWS_PER_STEP
    assert b % rb == 0
    cur = lambda n: jnp.minimum(n, nc - 1)
    main = lambda part: pl.BlockSpec((rb, CHUNK, hd), lambda bi, n: (bi, cur(n), part))
    halo = lambda part: pl.BlockSpec((rb, 8, hd), lambda bi, n: (bi, jnp.maximum(cur(n) * (CHUNK // 8) - 1, 0), part))
    lag = lambda width, col: pl.BlockSpec((rb, CHUNK, width), lambda bi, n: (bi, jnp.maximum(n - 1, 0), col))
    slots = lambda width: pltpu.VMEM((2 * rb, CHUNK, width), F32)
    return pl.pallas_call(
        functools.partial(_gdn_pipe_kernel, heads=heads, pad=pad, nc=nc),
        grid=(b // rb, nc + 1),
        in_specs=[main(0), main(1), main(2), halo(0), halo(1), halo(2),
                  pl.BlockSpec((rb, CHUNK, LANES), lambda bi, n: (bi, cur(n), 0)),
                  pl.BlockSpec(conv_w.shape, lambda bi, n: (0, 0)),
                  pl.BlockSpec(gate_params.shape, lambda bi, n: (0, 0)),
                  lag(hd, 3), pl.BlockSpec((1, DN_DV), lambda bi, n: (0, 0))],
        out_specs=lag(hd, 0),
        out_shape=jax.ShapeDtypeStruct((b, lp, hd), BF16),
        scratch_shapes=[pltpu.VMEM((rb * heads, DN_DK, DN_DV), F32), pltpu.VMEM((3 * rb, 8 + CHUNK, hd), F32),
                        slots(hd), slots(hd), slots(hd), slots(hd), slots(heads * 2 * CHUNK),
                        pltpu.VMEM((2 * rb, SUBLANES, LANES), F32)],
        compiler_params=_params(("parallel", "arbitrary"), 48 * rb * CHUNK * hd * 4),
        name="gdn_pipe",
    )(qkvz, qkvz, qkvz, qkvz, qkvz, qkvz, ba, conv_w, gate_params, qkvz, norm_w.reshape(1, DN_DV))


CONV_HALO = 32
CONV_CT = LANES


def _conformer_kernel(a_ref, g_ref, ah_ref, gh_ref, cw_ref, cb_ref, lg_ref, lb_ref, o_ref, ext_ref, sh_ref, y_ref):
    r, ch = o_ref.shape[1], o_ref.shape[2]
    kw = cw_ref.shape[0]
    ext_ref[0:CONV_HALO, :] = ah_ref[0].astype(F32) * _sigmoid(gh_ref[0].astype(F32))
    ext_ref[CONV_HALO:CONV_HALO + r, :] = a_ref[0].astype(F32) * _sigmoid(g_ref[0].astype(F32))
    base = CONV_HALO - (kw - 1)
    span = sh_ref.shape[1]

    def ch_body(ci, carry):
        c0 = pl.multiple_of(ci * CONV_CT, CONV_CT)
        for s in range(1, SUBLANES):
            sh_ref[s - 1] = ext_ref[s:s + span, pl.ds(c0, CONV_CT)]
        acc = jnp.broadcast_to(cb_ref[:, pl.ds(c0, CONV_CT)], (r, CONV_CT))
        for j in range(kw):
            q, s = divmod(base + j, SUBLANES)
            rows = slice(q * SUBLANES, q * SUBLANES + r)
            tap = sh_ref[s - 1, rows, :] if s else ext_ref[rows, pl.ds(c0, CONV_CT)]
            acc = acc + cw_ref[j:j + 1, pl.ds(c0, CONV_CT)] * tap
        y_ref[:, pl.ds(c0, CONV_CT)] = acc
        return carry

    lax.fori_loop(0, ch // CONV_CT, ch_body, 0)
    y = y_ref[...]
    mu = jnp.mean(y, axis=-1, keepdims=True)
    yc = y - mu
    var = jnp.mean(yc * yc, axis=-1, keepdims=True)
    yn = yc * lax.rsqrt(var + EPS) * lg_ref[...] + lb_ref[...]
    o_ref[0] = _silu(yn).astype(o_ref.dtype)


def _conformer(tail, conv_w, conv_b, ln_g, ln_b):
    b, lp, _ = tail.shape
    ch = conv_w.shape[1]
    assert conv_w.shape[0] - 1 <= CONV_HALO and ch % CONV_CT == 0
    r = _pick(lp, 320, CONV_HALO)
    main = lambda col: pl.BlockSpec((1, r, ch), lambda bi, i: (bi, i, col))
    halo = lambda col: pl.BlockSpec((1, CONV_HALO, ch), lambda bi, i: (bi, jnp.maximum(i * (r // CONV_HALO) - 1, 0), col))
    vec = pl.BlockSpec((1, ch), lambda bi, i: (0, 0))
    return pl.pallas_call(
        _conformer_kernel,
        grid=(b, lp // r),
        in_specs=[main(0), main(1), halo(0), halo(1), pl.BlockSpec(conv_w.shape, lambda bi, i: (0, 0)), vec, vec, vec],
        out_specs=pl.BlockSpec((1, r, ch), lambda bi, i: (bi, i, 0)),
        out_shape=jax.ShapeDtypeStruct((b, lp, ch), BF16),
        scratch_shapes=[pltpu.VMEM((CONV_HALO + r, ch), F32),
                        pltpu.VMEM((SUBLANES - 1, CONV_HALO + r - SUBLANES, CONV_CT), F32),
                        pltpu.VMEM((r, ch), F32)],
        compiler_params=_params(("parallel", "arbitrary"), 14 * r * ch * 4),
        name="conformer",
    )(tail, tail, tail, tail, conv_w, conv_b.reshape(1, ch), ln_g.reshape(1, ch), ln_b.reshape(1, ch))


def _branch_kernel(od_ref, oc_ref, wd_ref, wc_ref, gd_ref, gc_ref, o_ref):
    yd = jnp.dot(od_ref[...], wd_ref[0].astype(BF16), preferred_element_type=F32)
    yc = jnp.dot(oc_ref[...], wc_ref[0].astype(BF16), preferred_element_type=F32)
    o_ref[...] = (_sigmoid(gd_ref[...].astype(F32)) * yd + _sigmoid(gc_ref[...].astype(F32)) * yc).astype(o_ref.dtype)


def _branches(o_dn, o_cv, w_dn_out, w_conv_out, layer, tail, gate_col0):
    m, k = o_dn.shape
    d = w_dn_out.shape[2]
    assert o_cv.shape == o_dn.shape and w_conv_out.shape == w_dn_out.shape
    tn = min(512, d)
    tm = _pick(m, 1040, 16)
    g0 = gate_col0 // tn
    assert gate_col0 % tn == 0 and d % tn == 0
    x_spec = pl.BlockSpec((tm, k), lambda i, j: (i, 0))
    w_spec = pl.BlockSpec((1, k, tn), lambda i, j: (layer, 0, j))
    vmem = 2 * (2 * tm * k * 2 + 2 * k * tn * 4 + 2 * tm * tn * 4 + tm * tn * 2) + 2 * k * tn * 2 + 4 * tm * tn * 4
    return pl.pallas_call(
        _branch_kernel,
        grid=(m // tm, d // tn),
        in_specs=[x_spec, x_spec, w_spec, w_spec,
                  pl.BlockSpec((tm, tn), lambda i, j: (i, g0 + j)),
                  pl.BlockSpec((tm, tn), lambda i, j: (i, g0 + d // tn + j))],
        out_specs=pl.BlockSpec((tm, tn), lambda i, j: (i, j)),
        out_shape=jax.ShapeDtypeStruct((m, d), BF16),
        compiler_params=_params(("parallel", "arbitrary"), vmem + (4 << 20)),
        name="branches",
    )(o_dn, o_cv, w_dn_out, w_conv_out, tail, tail)


def _ffn_kernel(x_ref, w1_ref, w3_ref, w2_ref, o_ref):
    @pl.when(pl.program_id(1) == 0)
    def _():
        o_ref[...] = jnp.zeros_like(o_ref)

    x = x_ref[...]
    h1 = jnp.dot(x, w1_ref[0].astype(BF16), preferred_element_type=F32)
    h3 = jnp.dot(x, w3_ref[0].astype(BF16), preferred_element_type=F32)
    act = (_silu(h1) * h3).astype(BF16)
    o_ref[...] += jnp.dot(act, w2_ref[0].astype(BF16), preferred_element_type=F32)


def _ffn(x, w1, w3, w2, layer):
    m, d = x.shape
    ff = w1.shape[2]
    tm = _pick(m, 1040, 16)
    tf = _pick(ff, 256, LANES)
    vmem = 2 * (tm * d * 2 + 3 * d * tf * 4 + tm * d * 4) + 3 * d * tf * 2 + 4 * tm * tf * 4 + tm * d * 4
    return pl.pallas_call(
        _ffn_kernel,
        grid=(m // tm, ff // tf),
        in_specs=[pl.BlockSpec((tm, d), lambda i, f: (i, 0)),
                  pl.BlockSpec((1, d, tf), lambda i, f: (layer, 0, f)),
                  pl.BlockSpec((1, d, tf), lambda i, f: (layer, 0, f)),
                  pl.BlockSpec((1, tf, d), lambda i, f: (layer, f, 0))],
        out_specs=pl.BlockSpec((tm, d), lambda i, f: (i, 0)),
        out_shape=jax.ShapeDtypeStruct((m, d), F32),
        compiler_params=_params(("parallel", "arbitrary"), vmem + (4 << 20)),
        name="ffn",
    )(x, w1, w3, w2)


def _router_kernel(h_ref, nw_ref, rw_ref, u_ref, wt_ref, ix_ref, cnt_ref, *, n_exp):
    @pl.when(pl.program_id(0) == 0)
    def _():
        cnt_ref[...] = jnp.zeros_like(cnt_ref)

    h = h_ref[...]
    tr = h.shape[0]
    u = h * lax.rsqrt(jnp.mean(h * h, axis=-1, keepdims=True) + EPS) * nw_ref[...]
    u_ref[...] = u.astype(u_ref.dtype)
    logits = jnp.dot(u, rw_ref[...], precision=HI, preferred_element_type=F32)
    lane = lax.broadcasted_iota(jnp.int32, logits.shape, 1)
    logits = jnp.where(lane < n_exp, logits, NEG_BIG)
    m1 = jnp.max(logits, axis=-1, keepdims=True)
    i1 = jnp.min(jnp.where(logits == m1, lane, LANES), axis=-1, keepdims=True)
    rest = jnp.where(lane == i1, NEG_BIG, logits)
    m2 = jnp.max(rest, axis=-1, keepdims=True)
    i2 = jnp.min(jnp.where(rest == m2, lane, LANES), axis=-1, keepdims=True)
    e2 = jnp.exp(m2 - m1)
    den = 1.0 + e2
    wt_ref[...] = jnp.where(lane == 0, 1.0 / den, jnp.where(lane == 1, e2 / den, 0.0))

    chosen = jnp.where((lane == i1) | (lane == i2), 1.0, 0.0)
    ri = lax.broadcasted_iota(jnp.int32, (tr, tr), 0)
    ci = lax.broadcasted_iota(jnp.int32, (tr, tr), 1)
    before = jnp.where(ri > ci, 1.0, 0.0).astype(BF16)
    prefix = jnp.dot(before, chosen.astype(BF16), preferred_element_type=F32) + cnt_ref[0:1, :]
    ra = jnp.sum(jnp.where(lane == i1, prefix, 0.0), axis=-1, keepdims=True).astype(jnp.int32)
    rb = jnp.sum(jnp.where(lane == i2, prefix, 0.0), axis=-1, keepdims=True).astype(jnp.int32)
    ix_ref[...] = jnp.where(lane == 0, i1, jnp.where(lane == 1, i2, jnp.where(lane == 2, ra, jnp.where(lane == 3, rb, 0))))
    cnt_ref[...] = cnt_ref[...] + jnp.sum(chosen, axis=0, keepdims=True)


def _router(h, norm_w, router_w):
    m, d = h.shape
    n_exp = router_w.shape[1]
    assert 2 <= n_exp <= LANES
    rw = jnp.pad(router_w, ((0, 0), (0, LANES - n_exp)))
    tr = _pick(m, 520, 8)
    row_spec = pl.BlockSpec((tr, d), lambda i: (i, 0))
    lane_spec = pl.BlockSpec((tr, LANES), lambda i: (i, 0))
    return pl.pallas_call(
        functools.partial(_router_kernel, n_exp=n_exp),
        grid=(m // tr,),
        in_specs=[row_spec, pl.BlockSpec((1, d), lambda i: (0, 0)), pl.BlockSpec((d, LANES), lambda i: (0, 0))],
        out_specs=[row_spec, lane_spec, lane_spec, pl.BlockSpec((8, LANES), lambda i: (0, 0))],
        out_shape=[jax.ShapeDtypeStruct((m, d), BF16), jax.ShapeDtypeStruct((m, LANES), F32),
                   jax.ShapeDtypeStruct((m, LANES), jnp.int32), jax.ShapeDtypeStruct((8, LANES), F32)],
        compiler_params=_params(("arbitrary",), 12 * tr * d * 4),
        name="router",
    )(h, norm_w.reshape(1, d), rw)


MOE_TILE = 1088
MOE_SUBS = 4


def _moe_tile_rows(m):
    return MOE_TILE if m >= 4 * MOE_TILE else 32 * MOE_SUBS


def _dispatch_kernel(pa_ref, pb_ref, u_ref, z_ref, xs_ref, sem):
    del z_ref
    i = pl.program_id(0)
    tr = u_ref.shape[0]

    def copy(r, pos_ref):
        return pltpu.make_async_copy(u_ref.at[r], xs_ref.at[pos_ref[i * tr + r]], sem)

    def issue(r, carry):
        copy(r, pa_ref).start()
        copy(r, pb_ref).start()
        return carry

    lax.fori_loop(0, tr, issue, 0)
    for _ in range(2):
        pltpu.make_async_copy(u_ref, xs_ref.at[pl.ds(0, tr)], sem).wait()


def _dispatch(u3, pos_a, pos_b, n_rows):
    m, s, l = u3.shape
    tr = _pick(m, 520, 8)
    return pl.pallas_call(
        _dispatch_kernel,
        grid_spec=pltpu.PrefetchScalarGridSpec(
            num_scalar_prefetch=2, grid=(m // tr,),
            in_specs=[pl.BlockSpec((tr, s, l), lambda i, pa, pb: (i, 0, 0)), pl.BlockSpec(memory_space=pl.ANY)],
            out_specs=pl.BlockSpec(memory_space=pl.ANY),
            scratch_shapes=[pltpu.SemaphoreType.DMA]),
        out_shape=jax.ShapeDtypeStruct((n_rows, s, l), u3.dtype),
        input_output_aliases={3: 0},
        compiler_params=_params(("arbitrary",), 4 * tr * s * l * 2),
        name="moe_dispatch",
    )(pos_a, pos_b, u3, jnp.zeros((n_rows, s, l), u3.dtype))


def _moe_ffn_kernel(te_ref, rows_ref, src_ref, x_ref, w1_ref, w3_ref, w2_ref, o_ref, w1b_ref, w3b_ref, w2b_ref):
    del te_ref, src_ref
    s = pl.program_id(0)
    f = pl.program_id(1)
    rows = rows_ref[s]
    sub = x_ref.shape[0] // MOE_SUBS

    def block(i):
        sl = slice(i * sub, (i + 1) * sub)
        x = x_ref[sl, :]
        h1 = jnp.dot(x, w1b_ref[...], preferred_element_type=F32)
        h3 = jnp.dot(x, w3b_ref[...], preferred_element_type=F32)
        act = (_silu(h1) * h3).astype(BF16)
        o_ref[sl, :] += jnp.dot(act, w2b_ref[...], preferred_element_type=F32)

    @pl.when(f == 0)
    def _():
        o_ref[...] = jnp.zeros_like(o_ref)

    full = rows > (MOE_SUBS - 1) * sub

    @pl.when(full)
    def _():
        x = x_ref[...]
        h1 = jnp.dot(x, w1_ref[0].astype(BF16), preferred_element_type=F32)
        h3 = jnp.dot(x, w3_ref[0].astype(BF16), preferred_element_type=F32)
        act = (_silu(h1) * h3).astype(BF16)
        o_ref[...] += jnp.dot(act, w2_ref[0].astype(BF16), preferred_element_type=F32)

    @pl.when(jnp.logical_and(rows > 0, jnp.logical_not(full)))
    def _():
        w1b_ref[...] = w1_ref[0].astype(BF16)
        w3b_ref[...] = w3_ref[0].astype(BF16)
        w2b_ref[...] = w2_ref[0].astype(BF16)
        block(0)

    for i in range(1, MOE_SUBS - 1):
        @pl.when(jnp.logical_and(rows > i * sub, jnp.logical_not(full)))
        def _():
            block(i)


def _moe_ffn(xs, w1, w3, w2, tile_expert, tile_rows, tile_src, r):
    p, d = xs.shape
    ff = w1.shape[2]
    tf = _pick(ff, 256, LANES)
    nf = ff // tf
    f_of = lambda s, f, rows: jnp.where(rows[s] > 0, f, nf - 1)
    vmem = 2 * (r * d * 2 + 3 * d * tf * 4 + r * d * 4) + 3 * d * tf * 2 + 6 * (r // MOE_SUBS) * tf * 4
    return pl.pallas_call(
        _moe_ffn_kernel,
        grid_spec=pltpu.PrefetchScalarGridSpec(
            num_scalar_prefetch=3, grid=(p // r, nf),
            in_specs=[pl.BlockSpec((r, d), lambda s, f, te, rows, src: (src[s], 0)),
                      pl.BlockSpec((1, d, tf), lambda s, f, te, rows, src: (te[s], 0, f_of(s, f, rows))),
                      pl.BlockSpec((1, d, tf), lambda s, f, te, rows, src: (te[s], 0, f_of(s, f, rows))),
                      pl.BlockSpec((1, tf, d), lambda s, f, te, rows, src: (te[s], f_of(s, f, rows), 0))],
            out_specs=pl.BlockSpec((r, d), lambda s, f, te, rows, src: (s, 0)),
            scratch_shapes=[pltpu.VMEM((d, tf), BF16), pltpu.VMEM((d, tf), BF16), pltpu.VMEM((tf, d), BF16)]),
        out_shape=jax.ShapeDtypeStruct((p, d), F32),
        compiler_params=_params(("arbitrary", "arbitrary"), vmem + (4 << 20)),
        name="moe_ffn",
    )(tile_expert, tile_rows, tile_src, xs, w1, w3, w2)


def _combine_kernel(pa_ref, pb_ref, wt_ref, ys_ref, o_ref, bufa_ref, bufb_ref, sem):
    i = pl.program_id(0)
    tr = o_ref.shape[0]

    def copy(r, pos_ref, buf_ref):
        return pltpu.make_async_copy(ys_ref.at[pl.ds(pos_ref[i * tr + r], 1), :], buf_ref.at[pl.ds(r, 1), :], sem)

    def issue(r, carry):
        copy(r, pa_ref, bufa_ref).start()
        copy(r, pb_ref, bufb_ref).start()
        return carry

    lax.fori_loop(0, tr, issue, 0)
    pltpu.make_async_copy(ys_ref.at[pl.ds(0, tr), :], bufa_ref, sem).wait()
    pltpu.make_async_copy(ys_ref.at[pl.ds(0, tr), :], bufb_ref, sem).wait()
    o_ref[...] = wt_ref[:, 0:1] * bufa_ref[...] + wt_ref[:, 1:2] * bufb_ref[...]


def _combine(ys, wt, pos_a, pos_b):
    m = wt.shape[0]
    d = ys.shape[1]
    tr = _pick(m, 520, 8)
    return pl.pallas_call(
        _combine_kernel,
        grid_spec=pltpu.PrefetchScalarGridSpec(
            num_scalar_prefetch=2, grid=(m // tr,),
            in_specs=[pl.BlockSpec((tr, LANES), lambda i, pa, pb: (i, 0)), pl.BlockSpec(memory_space=pl.ANY)],
            out_specs=pl.BlockSpec((tr, d), lambda i, pa, pb: (i, 0)),
            scratch_shapes=[pltpu.VMEM((tr, d), F32), pltpu.VMEM((tr, d), F32), pltpu.SemaphoreType.DMA]),
        out_shape=jax.ShapeDtypeStruct((m, d), F32),
        compiler_params=_params(("arbitrary",), 8 * tr * d * 4),
        name="moe_combine",
    )(pos_a, pos_b, wt, ys)


def _moe(h, norm_w, router_w, w1, w3, w2, e0):
    m, d = h.shape
    n_exp = router_w.shape[1]
    r = _moe_tile_rows(m)
    n_tiles = (2 * m + n_exp * (r - 1) + r - 1) // r
    u, wt, ix, cnt = _router(h, norm_w, router_w)

    cnt = cnt[0, :n_exp].astype(jnp.int32)
    nt = (cnt + r - 1) // r
    tend = jnp.cumsum(nt)
    tstart = tend - nt
    pos_a = tstart[ix[:, 0]] * r + ix[:, 2]
    pos_b = tstart[ix[:, 1]] * r + ix[:, 3]
    n_used = tend[n_exp - 1]
    s = jnp.arange(n_tiles, dtype=jnp.int32)
    tile_src = jnp.clip(s, 0, jnp.maximum(n_used - 1, 0))
    tile_expert = jnp.minimum(jnp.searchsorted(tend, tile_src, side="right"), n_exp - 1).astype(jnp.int32)
    tile_rows = jnp.where(s < n_used, jnp.clip(cnt[tile_expert] - (s - tstart[tile_expert]) * r, 0, r), 0).astype(jnp.int32)

    assert d % LANES == 0
    xs = _dispatch(u.reshape(m, d // LANES, LANES), pos_a, pos_b, n_tiles * r)
    ys = _moe_ffn(xs.reshape(n_tiles * r, d), w1, w3, w2, tile_expert + e0, tile_rows, tile_src, r)
    return _combine(ys, wt, pos_a, pos_b)


def kernel(x, meta_tokens, attn_norm, w_in, short_conv_w, a_log, dt_bias, dn_norm, w_dn_out, dw_conv_w, dw_conv_b,
           conv_ln_g, conv_ln_b, w_conv_out, w_merge_out, ffn_norm, dense_w1, dense_w3, dense_w2, router_w,
           moe_w1, moe_w3, moe_w2, final_norm):
    bsz, seq, d = x.shape
    n_meta = meta_tokens.shape[0]
    depth, heads = a_log.shape
    pad = (-n_meta) % CHUNK
    lp = pad + n_meta + seq
    m = bsz * lp
    hd = heads * DN_DK
    ch = dw_conv_w.shape[2]
    assert lp % CHUNK == 0 and DN_DK == DN_DV == LANES and heads <= LANES
    off_b = 3 * hd + heads * DN_DV
    off_glu = off_b + 2 * heads
    assert w_in.shape[2] == off_glu + 2 * ch + 2 * d

    w_t = jnp.swapaxes(w_in, 1, 2)
    assert off_b + LANES <= w_in.shape[2] and 2 * heads <= LANES

    meta = jnp.broadcast_to(meta_tokens[None].astype(x.dtype), (bsz, n_meta, d))
    h = jnp.concatenate([jnp.zeros((bsz, pad, d), x.dtype), meta, x], axis=1).reshape(m, d)

    delta = None
    for i in range(depth):
        h_new, u = _addnorm(h, delta, attn_norm[i], lp=lp, pad=pad, want_h=delta is not None)
        h = h if delta is None else h_new
        qkvz = _mm_nt(u, w_t, i, 0, off_b, name="in_proj_qkvz")
        ba = _mm_nt(u, w_t, i, off_b, LANES, name="in_proj_ba")
        tail = _mm_nt(u, w_t, i, off_glu, 2 * ch + 2 * d, out_dtype=BF16,
                      name="in_proj_tail")
        gate_params = jnp.zeros((8, LANES), F32)
        gate_params = gate_params.at[0, 0:heads].set(a_log[i]).at[1, 0:heads].set(dt_bias[i])

        qkvz3 = qkvz.reshape(bsz, lp, off_b)
        tail3 = tail.reshape(bsz, lp, 2 * ch + 2 * d)
        o_dn = _gdn_pipe(qkvz3, ba.reshape(bsz, lp, LANES), short_conv_w[i], gate_params, dn_norm[i], heads=heads, pad=pad)
        o_cv = _conformer(tail3, dw_conv_w[i], dw_conv_b[i], conv_ln_g[i], conv_ln_b[i])
        merged = _branches(o_dn.reshape(m, hd), o_cv.reshape(m, ch), w_dn_out, w_conv_out, i, tail, 2 * ch)
        h = _mm(merged, w_merge_out, i, res=h, name="merge_out")

        j = i // 2
        if i % 2 == 0:
            _, u = _addnorm(h, None, ffn_norm[i], lp=lp, pad=0, want_h=False)
            delta = _ffn(u, dense_w1, dense_w3, dense_w2, j)
        else:
            n_exp = moe_w1.shape[1]
            flat = lambda t: t.reshape((-1,) + t.shape[2:])
            delta = _moe(h, ffn_norm[i], router_w[j], flat(moe_w1), flat(moe_w3), flat(moe_w2), j * n_exp)

    return _final_norm(h.reshape(bsz, lp, d), delta.reshape(bsz, lp, d), final_norm, skip=pad + n_meta)
```

```python
import functools

import jax
import jax.numpy as jnp
from jax import lax
from jax.experimental import pallas as pl
from jax.experimental.pallas import tpu as pltpu

DN_DK = 128
DN_DV = 128
CHUNK = 64
EPS = 1e-6
LANES = 128
SUBLANES = 8
NEG_BIG = -1e30
V7X_VMEM_BYTES = 64 * 1024 * 1024
VMEM_CAP = V7X_VMEM_BYTES - 8 * 1024 * 1024

F32 = jnp.float32
BF16 = jnp.bfloat16
HI = lax.Precision.HIGHEST


def _pick(n, target, mult):
    best = None
    for d in range(mult, min(n, target) + 1, mult):
        if n % d == 0:
            best = d
    assert best is not None, (n, target, mult)
    return best


def _params(sem, vmem_bytes):
    return pltpu.CompilerParams(dimension_semantics=sem,
                                vmem_limit_bytes=int(min(VMEM_CAP, max(vmem_bytes, 16 * 1024 * 1024))))


def _sigmoid(x):
    return 0.5 * jnp.tanh(0.5 * x) + 0.5


def _silu(x):
    return x * _sigmoid(x)


def _softplus(x):
    return jnp.maximum(x, 0.0) + jnp.log(1.0 + jnp.exp(-jnp.abs(x)))


def _addnorm_kernel(*refs, lp, pad, has_delta, want_h):
    it = iter(refs)
    h_ref = next(it)
    d_ref = next(it) if has_delta else None
    w_ref = next(it)
    hn_ref = next(it) if want_h else None
    u_ref = next(it)
    tr = h_ref.shape[0]
    h = h_ref[...]
    if has_delta:
        h = h + d_ref[...]
    if want_h:
        hn_ref[...] = h
    y = h * lax.rsqrt(jnp.mean(h * h, axis=-1, keepdims=True) + EPS) * w_ref[...]
    row = pl.program_id(0) * tr + lax.broadcasted_iota(jnp.int32, (tr, 1), 0)
    y = jnp.where(row % lp >= pad, y, 0.0)
    u_ref[...] = y.astype(u_ref.dtype)


def _addnorm(h, delta, w, *, lp, pad, want_h):
    m, d = h.shape
    tr = _pick(m, 520, 8)
    has_delta = delta is not None
    row_spec = pl.BlockSpec((tr, d), lambda i: (i, 0))
    in_specs = [row_spec] + ([row_spec] if has_delta else []) + [pl.BlockSpec((1, d), lambda i: (0, 0))]
    out_shape = ([jax.ShapeDtypeStruct((m, d), F32)] if want_h else []) + [jax.ShapeDtypeStruct((m, d), BF16)]
    out_specs = ([row_spec] if want_h else []) + [row_spec]
    args = [h] + ([delta] if has_delta else []) + [w.reshape(1, d)]
    outs = pl.pallas_call(
        functools.partial(_addnorm_kernel, lp=lp, pad=pad, has_delta=has_delta, want_h=want_h),
        grid=(m // tr,), in_specs=in_specs, out_specs=out_specs, out_shape=out_shape,
        compiler_params=_params(("parallel",), 12 * tr * d * 4),
        name="addnorm",
    )(*args)
    return outs if want_h else (None, outs[0])


def _final_norm_kernel(h_ref, d_ref, w_ref, o_ref):
    h = h_ref[0] + d_ref[0]
    o_ref[0] = h * lax.rsqrt(jnp.mean(h * h, axis=-1, keepdims=True) + EPS) * w_ref[...]


def _final_norm(h, delta, w, *, skip):
    b, lp, d = h.shape
    seq = lp - skip
    tr = _pick(seq, 512, SUBLANES)
    assert skip % SUBLANES == 0
    in_spec = pl.BlockSpec((pl.Element(1), pl.Element(tr), pl.Element(d)),
                           lambda bi, i: (bi, pl.multiple_of(skip + i * tr, SUBLANES), 0))
    return pl.pallas_call(
        _final_norm_kernel,
        grid=(b, seq // tr),
        in_specs=[in_spec, in_spec, pl.BlockSpec((1, d), lambda bi, i: (0, 0))],
        out_specs=pl.BlockSpec((1, tr, d), lambda bi, i: (bi, i, 0)),
        out_shape=jax.ShapeDtypeStruct((b, seq, d), F32),
        compiler_params=_params(("parallel", "parallel"), 12 * tr * d * 4),
        name="final_norm",
    )(h, delta, w.reshape(1, d))


def _mm_kernel(x_ref, w_ref, *rest):
    o_ref = rest[-1]
    y = jnp.dot(x_ref[...], w_ref[0].astype(BF16), preferred_element_type=F32)
    if len(rest) == 2:
        y = rest[0][...] + y
    o_ref[...] = y.astype(o_ref.dtype)


def _mm(x, w, layer, *, n_out=None, res=None, tn=512, out_dtype=F32, name="mm"):
    m, k = x.shape
    n_out = w.shape[2] if n_out is None else n_out
    tn = min(tn, n_out)
    assert n_out % tn == 0
    tm = _pick(m, 2080, 16)
    vmem = 2 * (tm * k * 2 + k * tn * 4 + 2 * tm * tn * 4) + k * tn * 2 + tm * tn * 4
    out_spec = pl.BlockSpec((tm, tn), lambda i, j: (i, j))
    return pl.pallas_call(
        _mm_kernel,
        grid=(m // tm, n_out // tn),
        in_specs=[pl.BlockSpec((tm, k), lambda i, j: (i, 0)),
                  pl.BlockSpec((1, k, tn), lambda i, j: (layer, 0, j))] + ([] if res is None else [out_spec]),
        out_specs=out_spec,
        out_shape=jax.ShapeDtypeStruct((m, n_out), out_dtype),
        compiler_params=_params(("parallel", "arbitrary"), vmem + (4 << 20)),
        name=name,
    )(*((x, w) if res is None else (x, w, res)))


def _mm_nt_kernel(x_ref, wt_ref, o_ref):
    y = lax.dot_general(x_ref[...], wt_ref[0].astype(BF16), (((1,), (1,)), ((), ())), preferred_element_type=F32)
    o_ref[...] = y.astype(o_ref.dtype)


def _mm_nt(x, wt, layer, row0, n_out, *, tn=512, out_dtype=F32, name="mm_nt"):
    m, k = x.shape
    tn = min(tn, n_out)
    assert n_out % tn == 0 and row0 % SUBLANES == 0 and row0 + n_out <= wt.shape[1]
    tm = _pick(m, 2080, 16)
    vmem = 2 * (tm * k * 2 + k * tn * 4 + tm * tn * 4) + 2 * k * tn * 2 + tm * tn * 4
    return pl.pallas_call(
        _mm_nt_kernel,
        grid=(m // tm, n_out // tn),
        in_specs=[pl.BlockSpec((tm, k), lambda i, j: (i, 0)),
                  pl.BlockSpec((pl.Element(1), pl.Element(tn), pl.Element(k)), lambda i, j: (layer, pl.multiple_of(row0 + j * tn, SUBLANES), 0))],
        out_specs=pl.BlockSpec((tm, tn), lambda i, j: (i, j)),
        out_shape=jax.ShapeDtypeStruct((m, n_out), out_dtype),
        compiler_params=_params(("parallel", "arbitrary"), vmem + (4 << 20)),
        name=name,
    )(x, wt)


GDN_ROWS_PER_STEP = 1


def _gdn_pipe_kernel(q_ref, k_ref, v_ref, qh_ref, kh_ref, vh_ref, ba_ref, cw_ref, gp_ref, z_ref, nw_ref,
                     o_ref, s_ref, ext_ref, pu_ref, pw_ref, pqd_ref, pkd_ref, pqk_ref, pcd_ref, *, heads, pad, nc):
    c = CHUNK
    hd = heads * DN_DK
    bsz = q_ref.shape[0]
    n = pl.program_id(1)
    slot = n % 2
    chunk = jnp.minimum(n, nc - 1)

    @pl.when(n == 0)
    def _():
        for ref in (s_ref, pu_ref, pw_ref, pqd_ref, pkd_ref, pqk_ref, pcd_ref):
            ref[...] = jnp.zeros_like(ref)

    for b in range(bsz):
        prev = (1 - slot) * bsz + b
        cd = pcd_ref[prev][0:1, :]
        for h in range(heads):
            sl = slice(h * DN_DV, (h + 1) * DN_DV)
            s = s_ref[b * heads + h]
            ws_qs = jnp.dot(jnp.concatenate([pw_ref[prev, :, sl], pqd_ref[prev, :, sl]], axis=0).astype(BF16),
                            s.astype(BF16), preferred_element_type=F32)
            v_new = (pu_ref[prev, :, sl] - ws_qs[0:c]).astype(BF16)
            o = ws_qs[c:2 * c] + jnp.dot(pqk_ref[prev, :, h * 2 * c:h * 2 * c + c].astype(BF16), v_new,
                                         preferred_element_type=F32)
            s_ref[b * heads + h] = s * cd[:, h:h + 1] + lax.dot_general(
                pkd_ref[prev, :, sl].astype(BF16), v_new, (((0,), (0,)), ((), ())), preferred_element_type=F32)
            on = o * lax.rsqrt(jnp.mean(o * o, axis=-1, keepdims=True) + EPS) * nw_ref[...]
            o_ref[b, :, sl] = (on * _silu(z_ref[b, :, sl])).astype(o_ref.dtype)

    def conv_silu(main_ref, halo_ref, b, part):
        e = b * 3 + part
        ext_ref[e, 0:8, :] = halo_ref[b]
        ext_ref[e, 8:8 + c, :] = main_ref[b]
        kw = cw_ref.shape[0]
        acc = jnp.zeros((c, hd), F32)
        for j in range(kw):
            acc = acc + cw_ref[j:j + 1, part * hd:(part + 1) * hd] * ext_ref[e, 8 - (kw - 1) + j:8 - (kw - 1) + j + c, :]
        return _silu(acc)

    def bdot(a, b):
        return jnp.dot(a.astype(BF16), b.astype(BF16), preferred_element_type=F32)

    row = chunk * c + lax.broadcasted_iota(jnp.int32, (c, 1), 0)
    live = row >= pad
    ii = lax.broadcasted_iota(jnp.int32, (2 * c, c), 0)
    jj = lax.broadcasted_iota(jnp.int32, (2 * c, c), 1)
    tri_pad = jnp.where((ii >= jj) & (ii < c), 1.0, 0.0).astype(F32)
    i2 = lax.broadcasted_iota(jnp.int32, (c, 2 * c), 0)
    j2 = lax.broadcasted_iota(jnp.int32, (c, 2 * c), 1)
    causal = i2 >= j2
    i1 = lax.broadcasted_iota(jnp.int32, (c, c), 0)
    j1 = lax.broadcasted_iota(jnp.int32, (c, c), 1)
    strict = i1 > j1

    for b in range(bsz):
        cur = slot * bsz + b
        ba = ba_ref[b]
        alpha = pltpu.roll(ba, LANES - heads, axis=1)
        beta = jnp.where(live, _sigmoid(ba), 0.0)
        g = jnp.where(live, -jnp.exp(gp_ref[0:1, :]) * _softplus(alpha + gp_ref[1:2, :]), 0.0)
        gc_pad = jnp.dot(tri_pad, g, precision=HI, preferred_element_type=F32)
        gc = gc_pad[0:c]
        gct = gc_pad.T
        eg = jnp.exp(gc)
        erev = jnp.exp(gc[c - 1:c, :] - gc)
        pcd_ref[cur] = jnp.broadcast_to(jnp.exp(gc[c - 1:c, :]), (SUBLANES, LANES))

        qa = conv_silu(q_ref, qh_ref, b, 0)
        ka = conv_silu(k_ref, kh_ref, b, 1)
        va = conv_silu(v_ref, vh_ref, b, 2)

        a_list, rhs_list = [], []
        for h in range(heads):
            sl = slice(h * DN_DK, (h + 1) * DN_DK)
            qh = qa[:, sl]
            kh = ka[:, sl]
            qh = qh * lax.rsqrt(jnp.sum(qh * qh, axis=-1, keepdims=True) + EPS) * (DN_DK ** -0.5)
            kh = kh * lax.rsqrt(jnp.sum(kh * kh, axis=-1, keepdims=True) + EPS)
            bh = beta[:, h:h + 1]
            egh = eg[:, h:h + 1]
            kb = kh * bh
            decay = jnp.exp(jnp.where(causal, gc[:, h:h + 1] - gct[h:h + 1, :], NEG_BIG))
            k_pad = jnp.concatenate([kh, jnp.zeros_like(kh)], axis=0)
            lhs = jnp.concatenate([kb, qh], axis=0).astype(BF16)
            kq = lax.dot_general(lhs, k_pad.astype(BF16), (((1,), (1,)), ((), ())), preferred_element_type=F32)
            a_list.append(jnp.where(strict, kq[0:c, 0:c] * decay[:, 0:c], 0.0))
            pqk_ref[cur, :, h * 2 * c:(h + 1) * 2 * c] = kq[c:2 * c, :] * decay
            rhs_list.append(jnp.concatenate([va[:, sl] * bh, kb * egh], axis=1))
            pqd_ref[cur, :, sl] = qh * egh
            pkd_ref[cur, :, sl] = kh * erev[:, h:h + 1]

        m_list = [-a for a in a_list]
        p_list = a_list
        span = 2
        while span < c:
            p_list = [bdot(p, p) for p in p_list]
            m_list = [mm + p + bdot(mm, p) for mm, p in zip(m_list, p_list)]
            span *= 2
        for h in range(heads):
            sl = slice(h * DN_DK, (h + 1) * DN_DK)
            sol = rhs_list[h] + bdot(m_list[h], rhs_list[h])
            pu_ref[cur, :, sl] = sol[:, 0:DN_DV]
            pw_ref[cur, :, sl] = sol[:, DN_DV:DN_DV + DN_DK]


def _gdn_pipe(qkvz, ba, conv_w, gate_params, norm_w, *, heads, pad):
    b, lp, _ = qkvz.shape
    hd = heads * DN_DK
    nc = lp // CHUNK
    rb = GDN_ROWS_PER_STEP
    assert b % rb == 0
    cur = lambda n: jnp.minimum(n, nc - 1)
    main = lambda part: pl.BlockSpec((rb, CHUNK, hd), lambda bi, n: (bi, cur(n), part))
    halo = lambda part: pl.BlockSpec((rb, 8, hd), lambda bi, n: (bi, jnp.maximum(cur(n) * (CHUNK // 8) - 1, 0), part))
    lag = lambda width, col: pl.BlockSpec((rb, CHUNK, width), lambda bi, n: (bi, jnp.maximum(n - 1, 0), col))
    slots = lambda width: pltpu.VMEM((2 * rb, CHUNK, width), F32)
    return pl.pallas_call(
        functools.partial(_gdn_pipe_kernel, heads=heads, pad=pad, nc=nc),
        grid=(b // rb, nc + 1),
        in_specs=[main(0), main(1), main(2), halo(0), halo(1), halo(2),
                  pl.BlockSpec((rb, CHUNK, LANES), lambda bi, n: (bi, cur(n), 0)),
                  pl.BlockSpec(conv_w.shape, lambda bi, n: (0, 0)),
                  pl.BlockSpec(gate_params.shape, lambda bi, n: (0, 0)),
                  lag(hd, 3), pl.BlockSpec((1, DN_DV), lambda bi, n: (0, 0))],
        out_specs=lag(hd, 0),
        out_shape=jax.ShapeDtypeStruct((b, lp, hd), BF16),
        scratch_shapes=[pltpu.VMEM((rb * heads, DN_DK, DN_DV), F32), pltpu.VMEM((3 * rb, 8 + CHUNK, hd), F32),
                        slots(hd), slots(hd), slots(hd), slots(hd), slots(heads * 2 * CHUNK),
                        pltpu.VMEM((2 * rb, SUBLANES, LANES), F32)],
        compiler_params=_params(("parallel", "arbitrary"), 48 * rb * CHUNK * hd * 4),
        name="gdn_pipe",
    )(qkvz, qkvz, qkvz, qkvz, qkvz, qkvz, ba, conv_w, gate_params, qkvz, norm_w.reshape(1, DN_DV))


CONV_HALO = 32
CONV_CT = LANES


def _conformer_kernel(a_ref, g_ref, ah_ref, gh_ref, cw_ref, cb_ref, lg_ref, lb_ref, o_ref, ext_ref, sh_ref, y_ref):
    r, ch = o_ref.shape[1], o_ref.shape[2]
    kw = cw_ref.shape[0]
    ext_ref[0:CONV_HALO, :] = ah_ref[0].astype(F32) * _sigmoid(gh_ref[0].astype(F32))
    ext_ref[CONV_HALO:CONV_HALO + r, :] = a_ref[0].astype(F32) * _sigmoid(g_ref[0].astype(F32))
    base = CONV_HALO - (kw - 1)
    span = sh_ref.shape[1]

    def ch_body(ci, carry):
        c0 = pl.multiple_of(ci * CONV_CT, CONV_CT)
        for s in range(1, SUBLANES):
            sh_ref[s - 1] = ext_ref[s:s + span, pl.ds(c0, CONV_CT)]
        acc = jnp.broadcast_to(cb_ref[:, pl.ds(c0, CONV_CT)], (r, CONV_CT))
        for j in range(kw):
            q, s = divmod(base + j, SUBLANES)
            rows = slice(q * SUBLANES, q * SUBLANES + r)
            tap = sh_ref[s - 1, rows, :] if s else ext_ref[rows, pl.ds(c0, CONV_CT)]
            acc = acc + cw_ref[j:j + 1, pl.ds(c0, CONV_CT)] * tap
        y_ref[:, pl.ds(c0, CONV_CT)] = acc
        return carry

    lax.fori_loop(0, ch // CONV_CT, ch_body, 0)
    y = y_ref[...]
    mu = jnp.mean(y, axis=-1, keepdims=True)
    yc = y - mu
    var = jnp.mean(yc * yc, axis=-1, keepdims=True)
    yn = yc * lax.rsqrt(var + EPS) * lg_ref[...] + lb_ref[...]
    o_ref[0] = _silu(yn).astype(o_ref.dtype)


def _conformer(tail, conv_w, conv_b, ln_g, ln_b):
    b, lp, _ = tail.shape
    ch = conv_w.shape[1]
    assert conv_w.shape[0] - 1 <= CONV_HALO and ch % CONV_CT == 0
    r = _pick(lp, 320, CONV_HALO)
    main = lambda col: pl.BlockSpec((1, r, ch), lambda bi, i: (bi, i, col))
    halo = lambda col: pl.BlockSpec((1, CONV_HALO, ch), lambda bi, i: (bi, jnp.maximum(i * (r // CONV_HALO) - 1, 0), col))
    vec = pl.BlockSpec((1, ch), lambda bi, i: (0, 0))
    return pl.pallas_call(
        _conformer_kernel,
        grid=(b, lp // r),
        in_specs=[main(0), main(1), halo(0), halo(1), pl.BlockSpec(conv_w.shape, lambda bi, i: (0, 0)), vec, vec, vec],
        out_specs=pl.BlockSpec((1, r, ch), lambda bi, i: (bi, i, 0)),
        out_shape=jax.ShapeDtypeStruct((b, lp, ch), BF16),
        scratch_shapes=[pltpu.VMEM((CONV_HALO + r, ch), F32),
                        pltpu.VMEM((SUBLANES - 1, CONV_HALO + r - SUBLANES, CONV_CT), F32),
                        pltpu.VMEM((r, ch), F32)],
        compiler_params=_params(("parallel", "arbitrary"), 14 * r * ch * 4),
        name="conformer",
    )(tail, tail, tail, tail, conv_w, conv_b.reshape(1, ch), ln_g.reshape(1, ch), ln_b.reshape(1, ch))


def _branch_kernel(od_ref, oc_ref, wd_ref, wc_ref, gd_ref, gc_ref, o_ref):
    yd = jnp.dot(od_ref[...], wd_ref[0].astype(BF16), preferred_element_type=F32)
    yc = jnp.dot(oc_ref[...], wc_ref[0].astype(BF16), preferred_element_type=F32)
    o_ref[...] = (_sigmoid(gd_ref[...].astype(F32)) * yd + _sigmoid(gc_ref[...].astype(F32)) * yc).astype(o_ref.dtype)


def _branches(o_dn, o_cv, w_dn_out, w_conv_out, layer, tail, gate_col0):
    m, k = o_dn.shape
    d = w_dn_out.shape[2]
    assert o_cv.shape == o_dn.shape and w_conv_out.shape == w_dn_out.shape
    tn = min(512, d)
    tm = _pick(m, 1040, 16)
    g0 = gate_col0 // tn
    assert gate_col0 % tn == 0 and d % tn == 0
    x_spec = pl.BlockSpec((tm, k), lambda i, j: (i, 0))
    w_spec = pl.BlockSpec((1, k, tn), lambda i, j: (layer, 0, j))
    vmem = 2 * (2 * tm * k * 2 + 2 * k * tn * 4 + 2 * tm * tn * 4 + tm * tn * 2) + 2 * k * tn * 2 + 4 * tm * tn * 4
    return pl.pallas_call(
        _branch_kernel,
        grid=(m // tm, d // tn),
        in_specs=[x_spec, x_spec, w_spec, w_spec,
                  pl.BlockSpec((tm, tn), lambda i, j: (i, g0 + j)),
                  pl.BlockSpec((tm, tn), lambda i, j: (i, g0 + d // tn + j))],
        out_specs=pl.BlockSpec((tm, tn), lambda i, j: (i, j)),
        out_shape=jax.ShapeDtypeStruct((m, d), BF16),
        compiler_params=_params(("parallel", "arbitrary"), vmem + (4 << 20)),
        name="branches",
    )(o_dn, o_cv, w_dn_out, w_conv_out, tail, tail)


def _ffn_kernel(x_ref, w1_ref, w3_ref, w2_ref, o_ref):
    @pl.when(pl.program_id(1) == 0)
    def _():
        o_ref[...] = jnp.zeros_like(o_ref)

    x = x_ref[...]
    h1 = jnp.dot(x, w1_ref[0].astype(BF16), preferred_element_type=F32)
    h3 = jnp.dot(x, w3_ref[0].astype(BF16), preferred_element_type=F32)
    act = (_silu(h1) * h3).astype(BF16)
    o_ref[...] += jnp.dot(act, w2_ref[0].astype(BF16), preferred_element_type=F32)


def _ffn(x, w1, w3, w2, layer):
    m, d = x.shape
    ff = w1.shape[2]
    tm = _pick(m, 1040, 16)
    tf = _pick(ff, 256, LANES)
    vmem = 2 * (tm * d * 2 + 3 * d * tf * 4 + tm * d * 4) + 3 * d * tf * 2 + 4 * tm * tf * 4 + tm * d * 4
    return pl.pallas_call(
        _ffn_kernel,
        grid=(m // tm, ff // tf),
        in_specs=[pl.BlockSpec((tm, d), lambda i, f: (i, 0)),
                  pl.BlockSpec((1, d, tf), lambda i, f: (layer, 0, f)),
                  pl.BlockSpec((1, d, tf), lambda i, f: (layer, 0, f)),
                  pl.BlockSpec((1, tf, d), lambda i, f: (layer, f, 0))],
        out_specs=pl.BlockSpec((tm, d), lambda i, f: (i, 0)),
        out_shape=jax.ShapeDtypeStruct((m, d), F32),
        compiler_params=_params(("parallel", "arbitrary"), vmem + (4 << 20)),
        name="ffn",
    )(x, w1, w3, w2)


def _router_kernel(h_ref, nw_ref, rw_ref, u_ref, wt_ref, ix_ref, cnt_ref, *, n_exp):
    @pl.when(pl.program_id(0) == 0)
    def _():
        cnt_ref[...] = jnp.zeros_like(cnt_ref)

    h = h_ref[...]
    tr = h.shape[0]
    u = h * lax.rsqrt(jnp.mean(h * h, axis=-1, keepdims=True) + EPS) * nw_ref[...]
    u_ref[...] = u.astype(u_ref.dtype)
    logits = jnp.dot(u, rw_ref[...], precision=HI, preferred_element_type=F32)
    lane = lax.broadcasted_iota(jnp.int32, logits.shape, 1)
    logits = jnp.where(lane < n_exp, logits, NEG_BIG)
    m1 = jnp.max(logits, axis=-1, keepdims=True)
    i1 = jnp.min(jnp.where(logits == m1, lane, LANES), axis=-1, keepdims=True)
    rest = jnp.where(lane == i1, NEG_BIG, logits)
    m2 = jnp.max(rest, axis=-1, keepdims=True)
    i2 = jnp.min(jnp.where(rest == m2, lane, LANES), axis=-1, keepdims=True)
    e2 = jnp.exp(m2 - m1)
    den = 1.0 + e2
    wt_ref[...] = jnp.where(lane == 0, 1.0 / den, jnp.where(lane == 1, e2 / den, 0.0))

    chosen = jnp.where((lane == i1) | (lane == i2), 1.0, 0.0)
    ri = lax.broadcasted_iota(jnp.int32, (tr, tr), 0)
    ci = lax.broadcasted_iota(jnp.int32, (tr, tr), 1)
    before = jnp.where(ri > ci, 1.0, 0.0).astype(BF16)
    prefix = jnp.dot(before, chosen.astype(BF16), preferred_element_type=F32) + cnt_ref[0:1, :]
    ra = jnp.sum(jnp.where(lane == i1, prefix, 0.0), axis=-1, keepdims=True).astype(jnp.int32)
    rb = jnp.sum(jnp.where(lane == i2, prefix, 0.0), axis=-1, keepdims=True).astype(jnp.int32)
    ix_ref[...] = jnp.where(lane == 0, i1, jnp.where(lane == 1, i2, jnp.where(lane == 2, ra, jnp.where(lane == 3, rb, 0))))
    cnt_ref[...] = cnt_ref[...] + jnp.sum(chosen, axis=0, keepdims=True)


def _router(h, norm_w, router_w):
    m, d = h.shape
    n_exp = router_w.shape[1]
    assert 2 <= n_exp <= LANES
    rw = jnp.pad(router_w, ((0, 0), (0, LANES - n_exp)))
    tr = _pick(m, 520, 8)
    row_spec = pl.BlockSpec((tr, d), lambda i: (i, 0))
    lane_spec = pl.BlockSpec((tr, LANES), lambda i: (i, 0))
    return pl.pallas_call(
        functools.partial(_router_kernel, n_exp=n_exp),
        grid=(m // tr,),
        in_specs=[row_spec, pl.BlockSpec((1, d), lambda i: (0, 0)), pl.BlockSpec((d, LANES), lambda i: (0, 0))],
        out_specs=[row_spec, lane_spec, lane_spec, pl.BlockSpec((8, LANES), lambda i: (0, 0))],
        out_shape=[jax.ShapeDtypeStruct((m, d), BF16), jax.ShapeDtypeStruct((m, LANES), F32),
                   jax.ShapeDtypeStruct((m, LANES), jnp.int32), jax.ShapeDtypeStruct((8, LANES), F32)],
        compiler_params=_params(("arbitrary",), 12 * tr * d * 4),
        name="router",
    )(h, norm_w.reshape(1, d), rw)


MOE_TILE = 1152
MOE_SUBS = 3


def _moe_tile_rows(m):
    return MOE_TILE if m >= 4 * MOE_TILE else 96


def _dispatch_kernel(pa_ref, pb_ref, u_ref, z_ref, xs_ref, sem):
    del z_ref
    i = pl.program_id(0)
    tr = u_ref.shape[0]

    def copy(r, pos_ref):
        return pltpu.make_async_copy(u_ref.at[r], xs_ref.at[pos_ref[i * tr + r]], sem)

    def issue(r, carry):
        copy(r, pa_ref).start(priority=0)
        copy(r, pb_ref).start(priority=1)
        return carry

    lax.fori_loop(0, tr, issue, 0)
    for _ in range(2):
        pltpu.make_async_copy(u_ref, xs_ref.at[pl.ds(0, tr)], sem).wait()


def _dispatch(u3, pos_a, pos_b, n_rows):
    m, s, l = u3.shape
    tr = _pick(m, 520, 8)
    return pl.pallas_call(
        _dispatch_kernel,
        grid_spec=pltpu.PrefetchScalarGridSpec(
            num_scalar_prefetch=2, grid=(m // tr,),
            in_specs=[pl.BlockSpec((tr, s, l), lambda i, pa, pb: (i, 0, 0)), pl.BlockSpec(memory_space=pl.ANY)],
            out_specs=pl.BlockSpec(memory_space=pl.ANY),
            scratch_shapes=[pltpu.SemaphoreType.DMA]),
        out_shape=jax.ShapeDtypeStruct((n_rows, s, l), u3.dtype),
        input_output_aliases={3: 0},
        compiler_params=_params(("arbitrary",), 4 * tr * s * l * 2),
        name="moe_dispatch",
    )(pos_a, pos_b, u3, jnp.zeros((n_rows, s, l), u3.dtype))


def _moe_ffn_kernel(te_ref, rows_ref, src_ref, x_ref, w1_ref, w3_ref, w2_ref, o_ref, w1b_ref, w3b_ref, w2b_ref):
    del te_ref, src_ref
    s = pl.program_id(0)
    f = pl.program_id(1)
    rows = rows_ref[s]
    sub = x_ref.shape[0] // MOE_SUBS

    def block(i):
        sl = slice(i * sub, (i + 1) * sub)
        x = x_ref[sl, :]
        h1 = jnp.dot(x, w1b_ref[...], preferred_element_type=F32)
        h3 = jnp.dot(x, w3b_ref[...], preferred_element_type=F32)
        act = (_silu(h1) * h3).astype(BF16)
        o_ref[sl, :] += jnp.dot(act, w2b_ref[...], preferred_element_type=F32)

    @pl.when(f == 0)
    def _():
        o_ref[...] = jnp.zeros_like(o_ref)

    full = rows > (MOE_SUBS - 1) * sub

    @pl.when(full)
    def _():
        x = x_ref[...]
        h1 = jnp.dot(x, w1_ref[0].astype(BF16), preferred_element_type=F32)
        h3 = jnp.dot(x, w3_ref[0].astype(BF16), preferred_element_type=F32)
        act = (_silu(h1) * h3).astype(BF16)
        o_ref[...] += jnp.dot(act, w2_ref[0].astype(BF16), preferred_element_type=F32)

    @pl.when(jnp.logical_and(rows > 0, jnp.logical_not(full)))
    def _():
        w1b_ref[...] = w1_ref[0].astype(BF16)
        w3b_ref[...] = w3_ref[0].astype(BF16)
        w2b_ref[...] = w2_ref[0].astype(BF16)
        block(0)

    for i in range(1, MOE_SUBS - 1):
        @pl.when(jnp.logical_and(rows > i * sub, jnp.logical_not(full)))
        def _():
            block(i)


def _moe_ffn(xs, w1, w3, w2, tile_expert, tile_rows, tile_src, r):
    p, d = xs.shape
    ff = w1.shape[2]
    tf = _pick(ff, 256, LANES)
    nf = ff // tf
    f_of = lambda s, f, rows: jnp.where(rows[s] > 0, f, nf - 1)
    vmem = 2 * (r * d * 2 + 3 * d * tf * 4 + r * d * 4) + 3 * d * tf * 2 + 6 * (r // MOE_SUBS) * tf * 4
    return pl.pallas_call(
        _moe_ffn_kernel,
        grid_spec=pltpu.PrefetchScalarGridSpec(
            num_scalar_prefetch=3, grid=(p // r, nf),
            in_specs=[pl.BlockSpec((r, d), lambda s, f, te, rows, src: (src[s], 0)),
                      pl.BlockSpec((1, d, tf), lambda s, f, te, rows, src: (te[s], 0, f_of(s, f, rows))),
                      pl.BlockSpec((1, d, tf), lambda s, f, te, rows, src: (te[s], 0, f_of(s, f, rows))),
                      pl.BlockSpec((1, tf, d), lambda s, f, te, rows, src: (te[s], f_of(s, f, rows), 0))],
            out_specs=pl.BlockSpec((r, d), lambda s, f, te, rows, src: (s, 0)),
            scratch_shapes=[pltpu.VMEM((d, tf), BF16), pltpu.VMEM((d, tf), BF16), pltpu.VMEM((tf, d), BF16)]),
        out_shape=jax.ShapeDtypeStruct((p, d), F32),
        compiler_params=_params(("arbitrary", "arbitrary"), vmem + (4 << 20)),
        name="moe_ffn",
    )(tile_expert, tile_rows, tile_src, xs, w1, w3, w2)


def _combine_kernel(pa_ref, pb_ref, wt_ref, ys_ref, o_ref, bufa_ref, bufb_ref, sem):
    i = pl.program_id(0)
    tr = o_ref.shape[0]

    def copy(r, pos_ref, buf_ref):
        return pltpu.make_async_copy(ys_ref.at[pl.ds(pos_ref[i * tr + r], 1), :], buf_ref.at[pl.ds(r, 1), :], sem)

    def issue(r, carry):
        copy(r, pa_ref, bufa_ref).start(priority=0)
        copy(r, pb_ref, bufb_ref).start(priority=1)
        return carry

    lax.fori_loop(0, tr, issue, 0)
    pltpu.make_async_copy(ys_ref.at[pl.ds(0, tr), :], bufa_ref, sem).wait()
    pltpu.make_async_copy(ys_ref.at[pl.ds(0, tr), :], bufb_ref, sem).wait()
    o_ref[...] = wt_ref[:, 0:1] * bufa_ref[...] + wt_ref[:, 1:2] * bufb_ref[...]


def _combine(ys, wt, pos_a, pos_b):
    m = wt.shape[0]
    d = ys.shape[1]
    tr = _pick(m, 520, 8)
    return pl.pallas_call(
        _combine_kernel,
        grid_spec=pltpu.PrefetchScalarGridSpec(
            num_scalar_prefetch=2, grid=(m // tr,),
            in_specs=[pl.BlockSpec((tr, LANES), lambda i, pa, pb: (i, 0)), pl.BlockSpec(memory_space=pl.ANY)],
            out_specs=pl.BlockSpec((tr, d), lambda i, pa, pb: (i, 0)),
            scratch_shapes=[pltpu.VMEM((tr, d), F32), pltpu.VMEM((tr, d), F32), pltpu.SemaphoreType.DMA]),
        out_shape=jax.ShapeDtypeStruct((m, d), F32),
        compiler_params=_params(("arbitrary",), 8 * tr * d * 4),
        name="moe_combine",
    )(pos_a, pos_b, wt, ys)


def _moe(h, norm_w, router_w, w1, w3, w2, e0):
    m, d = h.shape
    n_exp = router_w.shape[1]
    r = _moe_tile_rows(m)
    n_tiles = (2 * m + n_exp * (r - 1) + r - 1) // r
    u, wt, ix, cnt = _router(h, norm_w, router_w)

    cnt = cnt[0, :n_exp].astype(jnp.int32)
    nt = (cnt + r - 1) // r
    tend = jnp.cumsum(nt)
    tstart = tend - nt
    pos_a = tstart[ix[:, 0]] * r + ix[:, 2]
    pos_b = tstart[ix[:, 1]] * r + ix[:, 3]
    n_used = tend[n_exp - 1]
    s = jnp.arange(n_tiles, dtype=jnp.int32)
    tile_src = jnp.clip(s, 0, jnp.maximum(n_used - 1, 0))
    tile_expert = jnp.minimum(jnp.searchsorted(tend, tile_src, side="right"), n_exp - 1).astype(jnp.int32)
    tile_rows = jnp.where(s < n_used, jnp.clip(cnt[tile_expert] - (s - tstart[tile_expert]) * r, 0, r), 0).astype(jnp.int32)

    assert d % LANES == 0
    xs = _dispatch(u.reshape(m, d // LANES, LANES), pos_a, pos_b, n_tiles * r)
    ys = _moe_ffn(xs.reshape(n_tiles * r, d), w1, w3, w2, tile_expert + e0, tile_rows, tile_src, r)
    return _combine(ys, wt, pos_a, pos_b)


def kernel(x, meta_tokens, attn_norm, w_in, short_conv_w, a_log, dt_bias, dn_norm, w_dn_out, dw_conv_w, dw_conv_b,
           conv_ln_g, conv_ln_b, w_conv_out, w_merge_out, ffn_norm, dense_w1, dense_w3, dense_w2, router_w,
           moe_w1, moe_w3, moe_w2, final_norm):
    bsz, seq, d = x.shape
    n_meta = meta_tokens.shape[0]
    depth, heads = a_log.shape
    pad = (-n_meta) % CHUNK
    lp = pad + n_meta + seq
    m = bsz * lp
    hd = heads * DN_DK
    ch = dw_conv_w.shape[2]
    assert lp % CHUNK == 0 and DN_DK == DN_DV == LANES and heads <= LANES
    off_b = 3 * hd + heads * DN_DV
    off_glu = off_b + 2 * heads
    assert w_in.shape[2] == off_glu + 2 * ch + 2 * d

    w_t = jnp.swapaxes(w_in, 1, 2)
    assert off_b + LANES <= w_in.shape[2] and 2 * heads <= LANES

    meta = jnp.broadcast_to(meta_tokens[None].astype(x.dtype), (bsz, n_meta, d))
    h = jnp.concatenate([jnp.zeros((bsz, pad, d), x.dtype), meta, x], axis=1).reshape(m, d)

    delta = None
    for i in range(depth):
        h_new, u = _addnorm(h, delta, attn_norm[i], lp=lp, pad=pad, want_h=delta is not None)
        h = h if delta is None else h_new
        qkvz = _mm_nt(u, w_t, i, 0, off_b, name="in_proj_qkvz")
        ba = _mm_nt(u, w_t, i, off_b, LANES, name="in_proj_ba")
        tail = _mm_nt(u, w_t, i, off_glu, 2 * ch + 2 * d, out_dtype=BF16,
                      name="in_proj_tail")
        gate_params = jnp.zeros((8, LANES), F32)
        gate_params = gate_params.at[0, 0:heads].set(a_log[i]).at[1, 0:heads].set(dt_bias[i])

        qkvz3 = qkvz.reshape(bsz, lp, off_b)
        tail3 = tail.reshape(bsz, lp, 2 * ch + 2 * d)
        o_dn = _gdn_pipe(qkvz3, ba.reshape(bsz, lp, LANES), short_conv_w[i], gate_params, dn_norm[i], heads=heads, pad=pad)
        o_cv = _conformer(tail3, dw_conv_w[i], dw_conv_b[i], conv_ln_g[i], conv_ln_b[i])
        merged = _branches(o_dn.reshape(m, hd), o_cv.reshape(m, ch), w_dn_out, w_conv_out, i, tail, 2 * ch)
        h = _mm(merged, w_merge_out, i, res=h, name="merge_out")

        j = i // 2
        if i % 2 == 0:
            _, u = _addnorm(h, None, ffn_norm[i], lp=lp, pad=0, want_h=False)
            delta = _ffn(u, dense_w1, dense_w3, dense_w2, j)
        else:
            n_exp = moe_w1.shape[1]
            flat = lambda t: t.reshape((-1,) + t.shape[2:])
            delta = _moe(h, ffn_norm[i], router_w[j], flat(moe_w1), flat(moe_w3), flat(moe_w2), j * n_exp)

    return _final_norm(h.reshape(bsz, lp, d), delta.reshape(bsz, lp, d), final_norm, skip=pad + n_meta)
```
